```python
import jax, jax.numpy as jnp
from jax import lax
import numpy as np

D_MODEL = 1024
BATCH = 8
SEQ = 8192
DEPTH = 1

HEAD_DIM = 64
CONV_WIDTH_CH = 512
N_HEADS = 8
ATTN_WIDTH = N_HEADS * HEAD_DIM
MIX_WIDTH = CONV_WIDTH_CH + ATTN_WIDTH
CONV_K = 31
DILATED_PATTERNS = ((128, 1), (512, 4), (2048, 16))
Q_BLOCK = 128
D_FF = 2816
FFN_CONV_K = 3
EPS = 1e-6

kernel_name = "hymba_conformer_dilated_alibi_convffn"


def rms_norm(x, g):
    xf = x.astype(jnp.float32)
    y = xf * lax.rsqrt(jnp.mean(xf * xf, axis=-1, keepdims=True) + EPS)
    return (y * g.astype(jnp.float32)).astype(x.dtype)


def layer_norm(x, g, b):
    xf = x.astype(jnp.float32)
    mu = jnp.mean(xf, axis=-1, keepdims=True)
    var = jnp.mean(jnp.square(xf - mu), axis=-1, keepdims=True)
    y = (xf - mu) * lax.rsqrt(var + EPS)
    return (y * g.astype(jnp.float32) + b.astype(jnp.float32)).astype(x.dtype)


def causal_depthwise_conv(x, w, b):
    K, C = w.shape
    y = lax.conv_general_dilated(
        x, w[:, None, :].astype(x.dtype), window_strides=(1,), padding=[(K - 1, 0)],
        dimension_numbers=("NWC", "WIO", "NWC"), feature_group_count=C)
    return y + b.astype(x.dtype)


def alibi_slopes(n_heads):
    return 2.0 ** (-8.0 * jnp.arange(1, n_heads + 1, dtype=jnp.float32) / n_heads)


def dilated_window_attention(q, k, v, slopes, window, dilation):
    B, S, H, E = q.shape
    d = dilation
    n_back = window // d
    L = S // d
    nb = -(-L // Q_BLOCK)
    Lp = nb * Q_BLOCK

    def to_streams(a):
        a = a.reshape(B, L, d, H, E)
        a = jnp.pad(a, ((0, 0), (0, Lp - L), (0, 0), (0, 0), (0, 0)))
        return a.reshape(B, nb, Q_BLOCK, d, H, E)

    def with_prev(a):
        prev = jnp.concatenate([jnp.zeros_like(a[:, :1]), a[:, :-1]], axis=1)
        return jnp.concatenate([prev, a], axis=2)

    qs = to_streams(q)
    kk = with_prev(to_streams(k))
    vv = with_prev(to_streams(v))

    s = jnp.einsum("bnqrhe,bnkrhe->bnrhqk", qs, kk, preferred_element_type=jnp.float32)
    qi = jnp.arange(Q_BLOCK)[:, None]
    ki = jnp.arange(2 * Q_BLOCK)[None, :]
    delta = qi + Q_BLOCK - ki
    band = (delta >= 0) & (delta <= n_back)
    key_pos = jnp.arange(nb)[:, None] * Q_BLOCK - Q_BLOCK + jnp.arange(2 * Q_BLOCK)[None, :]
    key_ok = key_pos >= 0
    mask = band[None, :, :] & key_ok[:, None, :]
    dist = (delta * d).astype(jnp.float32)
    bias = -slopes[:, None, None] * dist[None]
    s = s + bias[None, None, None]
    s = jnp.where(mask[None, :, None, None], s, -jnp.inf)
    lse = jax.nn.logsumexp(s, axis=-1, keepdims=True)
    p = jnp.exp(s - lse)
    o = jnp.einsum("bnrhqk,bnkrhe->bnqrhe", p.astype(vv.dtype), vv,
                   preferred_element_type=jnp.float32)
    o = o.reshape(B, Lp, d, H, E)[:, :L].reshape(B, S, H, E)
    lse = jnp.transpose(lse[..., 0], (0, 1, 4, 2, 3))
    lse = lse.reshape(B, Lp, d, H)[:, :L].reshape(B, S, H)
    return o, lse


def setup_inputs(seed: int = 0) -> dict:
    key = jax.random.key(seed)
    ks = jax.random.split(key, 16)
    f32 = jnp.float32
    n_in = 2 * CONV_WIDTH_CH + 3 * ATTN_WIDTH
    nrm = lambda k, shape, fan: jax.random.normal(k, shape, f32) * (fan ** -0.5)
    gain = lambda k, n: 1.0 + 0.02 * jax.random.normal(k, (n,), f32)
    small = lambda k, n: 0.02 * jax.random.normal(k, (n,), f32)
    return {
        "x": jax.random.normal(ks[0], (BATCH, SEQ, D_MODEL), f32),
        "norm1_g": gain(ks[1], D_MODEL),
        "w_in": nrm(ks[2], (D_MODEL, n_in), D_MODEL),
        "conv_w": nrm(ks[3], (CONV_K, CONV_WIDTH_CH), CONV_K),
        "conv_b": small(ks[4], CONV_WIDTH_CH),
        "cn_g": gain(ks[5], CONV_WIDTH_CH),
        "cn_b": small(ks[6], CONV_WIDTH_CH),
        "q_norm_g": gain(ks[7], HEAD_DIM),
        "k_norm_g": gain(ks[8], HEAD_DIM),
        "w_out": nrm(ks[9], (MIX_WIDTH, D_MODEL), MIX_WIDTH),
        "norm2_g": gain(ks[10], D_MODEL),
        "w_up": nrm(ks[11], (D_MODEL, 2 * D_FF), D_MODEL),
        "ffconv_w": nrm(ks[12], (FFN_CONV_K, 2 * D_FF), FFN_CONV_K),
        "ffconv_b": small(ks[13], 2 * D_FF),
        "w_down": nrm(ks[14], (D_FF, D_MODEL), D_FF),
    }


def reference(x, norm1_g, w_in, conv_w, conv_b, cn_g, cn_b, q_norm_g, k_norm_g, w_out,
              norm2_g, w_up, ffconv_w, ffconv_b, w_down):
    B, S, _ = x.shape
    slopes = alibi_slopes(N_HEADS)
    for _layer in range(DEPTH):
        h = rms_norm(x, norm1_g)
        proj = h @ w_in
        c = CONV_WIDTH_CH
        a_val, a_gate, q, k, v = jnp.split(
            proj, [c, 2 * c, 2 * c + ATTN_WIDTH, 2 * c + 2 * ATTN_WIDTH], axis=-1)

        u = a_val * jax.nn.sigmoid(a_gate)
        u = causal_depthwise_conv(u, conv_w, conv_b)
        u = jax.nn.silu(layer_norm(u, cn_g, cn_b))

        q = rms_norm(q.reshape(B, S, N_HEADS, HEAD_DIM), q_norm_g) * (HEAD_DIM ** -0.5)
        k = rms_norm(k.reshape(B, S, N_HEADS, HEAD_DIM), k_norm_g)
        v = v.reshape(B, S, N_HEADS, HEAD_DIM)
        outs, lses = [], []
        for window, dilation in DILATED_PATTERNS:
            o_i, lse_i = dilated_window_attention(q, k, v, slopes, window, dilation)
            outs.append(o_i)
            lses.append(lse_i)
        wts = jax.nn.softmax(jnp.stack(lses, axis=0), axis=0)
        o = jnp.sum(wts[..., None] * jnp.stack(outs, axis=0), axis=0)
        o = o.astype(x.dtype).reshape(B, S, ATTN_WIDTH)

        x = x + jnp.concatenate([u, o], axis=-1) @ w_out

        h2 = rms_norm(x, norm2_g)
        up = causal_depthwise_conv(h2 @ w_up, ffconv_w, ffconv_b)
        gate, val = jnp.split(up, 2, axis=-1)
        x = x + (jax.nn.silu(gate) * val) @ w_down
    return x
```

```python
import functools

import jax
import jax.numpy as jnp
from jax import lax
from jax.experimental import pallas as pl
from jax.experimental.pallas import tpu as pltpu

F32 = jnp.float32
BF16 = jnp.bfloat16

D_MODEL = 1024
HEAD_DIM = 64
N_HEADS = 8
CONV_CH = 512
ATTN_W = N_HEADS * HEAD_DIM
CONV_K = 31
D_FF = 2816
FFN_K = 3
EPS = 1e-6
PATTERNS = ((128, 1), (512, 4), (2048, 16))
Q_BLOCK = 128
N_BACK = 128
NEG = -1e30

LANES = 128
ROW_TILE = 512
CONV_HALO = 32
CONV_ROWS = 32
FFN_HALO = 16
FFN_COLS = 256
SUPER = 16 * Q_BLOCK
VMEM_LIMIT = 52 * 1024 * 1024


def _resident(shape):
    return pl.BlockSpec(shape, lambda *_: (0,) * len(shape), pipeline_mode=pl.Buffered(1))


def _mix_in_kernel(x_ref, g1_ref, win_ref, cw_ref, cb_ref, cng_ref, cnb_ref, qg_ref, kg_ref,
                   hsum_ref,
                   u_ref, qn_ref, kn_ref, vn_ref, q4_ref, k4_ref, v4_ref, q16_ref, k16_ref,
                   v16_ref, gbuf, pbuf):
    ts = ROW_TILE
    i = pl.program_id(1)
    x = x_ref[0]
    h = (x * lax.rsqrt(jnp.mean(x * x, axis=-1, keepdims=True) + EPS) * g1_ref[...]).astype(BF16)

    @pl.when(i == 0)
    def _():
        gbuf[0:CONV_HALO, :] = jnp.zeros((CONV_HALO, CONV_CH), F32)

    a = jnp.dot(h, win_ref[:, 0:2 * CONV_CH], preferred_element_type=F32)
    gbuf[CONV_HALO:CONV_HALO + ts, :] = a[:, :CONV_CH] * jax.nn.sigmoid(a[:, CONV_CH:])

    for rc in range(ts // CONV_ROWS):
        base = rc * CONV_ROWS + CONV_HALO - (CONV_K - 1)
        acc = jnp.broadcast_to(cb_ref[...], (CONV_ROWS, CONV_CH))
        for k in range(CONV_K):
            acc = acc + cw_ref[k:k + 1, :] * gbuf[base + k:base + k + CONV_ROWS, :]
        mu = jnp.mean(acc, axis=-1, keepdims=True)
        d = acc - mu
        var = jnp.mean(d * d, axis=-1, keepdims=True)
        y = d * lax.rsqrt(var + EPS) * cng_ref[...] + cnb_ref[...]
        u_ref[0, rc * CONV_ROWS:(rc + 1) * CONV_ROWS, :] = (y * jax.nn.sigmoid(y)).astype(BF16)

    gbuf[0:CONV_HALO, :] = gbuf[ts:ts + CONV_HALO, :]

    def head_norm(col0, g_ref):
        z = jnp.dot(h, win_ref[:, col0:col0 + ATTN_W], preferred_element_type=F32)
        ss = jnp.dot((z * z).astype(BF16), hsum_ref[...], preferred_element_type=F32)
        return z * lax.rsqrt(ss * (1.0 / HEAD_DIM) + EPS) * g_ref[...]

    def emit(val, nat_ref, m4_ref, m16_ref):
        nat_ref[0] = val.astype(BF16)
        for c in range(ATTN_W // LANES):
            cols = slice(c * LANES, (c + 1) * LANES)
            pbuf[c] = val[:, cols]
            for r in range(4):
                m4_ref[0, r, :, cols] = pbuf[c, pl.ds(r, ts // 4, stride=4), :].astype(BF16)
            for r in range(16):
                m16_ref[0, r, :, cols] = pbuf[c, pl.ds(r, ts // 16, stride=16), :].astype(BF16)

    c0 = 2 * CONV_CH
    emit(head_norm(c0, qg_ref) * (HEAD_DIM ** -0.5), qn_ref, q4_ref, q16_ref)
    emit(head_norm(c0 + ATTN_W, kg_ref), kn_ref, k4_ref, k16_ref)
    emit(jnp.dot(h, win_ref[:, c0 + 2 * ATTN_W:c0 + 3 * ATTN_W], preferred_element_type=F32),
         vn_ref, v4_ref, v16_ref)


def _mix_in(x, g1, win, cw, cb, cng, cnb, qg, kg, hsum):
    B, S, _ = x.shape
    ts = ROW_TILE
    n_in = win.shape[1]
    tok = lambda b, i: (b, i, 0)
    strm = lambda b, i: (b, 0, i, 0)
    nat = jax.ShapeDtypeStruct((B, S, ATTN_W), BF16)
    m4 = jax.ShapeDtypeStruct((B, 4, S // 4, ATTN_W), BF16)
    m16 = jax.ShapeDtypeStruct((B, 16, S // 16, ATTN_W), BF16)
    nat_spec = pl.BlockSpec((1, ts, ATTN_W), tok)
    m4_spec = pl.BlockSpec((1, 4, ts // 4, ATTN_W), strm)
    m16_spec = pl.BlockSpec((1, 16, ts // 16, ATTN_W), strm)
    return pl.pallas_call(
        _mix_in_kernel,
        grid=(B, S // ts),
        in_specs=[
            pl.BlockSpec((1, ts, D_MODEL), tok),
            _resident((1, D_MODEL)),
            _resident((D_MODEL, n_in)),
            _resident((CONV_K, CONV_CH)),
            _resident((1, CONV_CH)),
            _resident((1, CONV_CH)),
            _resident((1, CONV_CH)),
            _resident((1, ATTN_W)),
            _resident((1, ATTN_W)),
            _resident((ATTN_W, ATTN_W)),
        ],
        out_specs=[pl.BlockSpec((1, ts, CONV_CH), tok),
                   nat_spec, nat_spec, nat_spec, m4_spec, m4_spec, m4_spec,
                   m16_spec, m16_spec, m16_spec],
        out_shape=[jax.ShapeDtypeStruct((B, S, CONV_CH), BF16),
                   nat, nat, nat, m4, m4, m4, m16, m16, m16],
        scratch_shapes=[pltpu.VMEM((ts + CONV_HALO, CONV_CH), F32),
                        pltpu.VMEM((ATTN_W // LANES, ts, LANES), F32)],
        compiler_params=pltpu.CompilerParams(
            dimension_semantics=("arbitrary", "arbitrary"), vmem_limit_bytes=VMEM_LIMIT),
        name="mix_in",
    )(x, g1, win, cw, cb, cng, cnb, qg, kg, hsum)


def _attn_kernel(slopes_ref, qn_ref, kn_ref, vn_ref, q4_ref, k4_ref, v4_ref, q16_ref, k16_ref,
                 v16_ref, o_ref, bias_ref, acc_s, m_s, l_s, *, seq):
    pair = pl.program_id(1)
    sb = pl.program_id(2)
    qb = Q_BLOCK
    lane = lax.broadcasted_iota(jnp.int32, (qb, LANES), 1)
    is_a = lane < HEAD_DIM
    head_mask = (is_a.astype(BF16), (~is_a).astype(BF16))

    @pl.when(sb == 0)
    def _():
        qi = lax.broadcasted_iota(jnp.int32, (qb, 2 * qb), 0)
        ki = lax.broadcasted_iota(jnp.int32, (qb, 2 * qb), 1)
        delta = qi + qb - ki
        band = (delta >= 0) & (delta <= N_BACK)
        for pat, (_, dil) in enumerate(PATTERNS):
            dist = (delta * dil).astype(F32)
            for hh in range(2):
                b = jnp.where(band, -slopes_ref[2 * pair + hh] * dist, NEG)
                bias_ref[pat, hh, 0] = b
                bias_ref[pat, hh, 1] = jnp.where(ki >= qb, b, NEG)

    def unit(q_ref, k_ref, v_ref, qrow0, first, pat):
        prow0 = pl.multiple_of(jnp.maximum(qrow0 - qb, 0), qb)
        qrow0 = pl.multiple_of(qrow0, qb)
        q = q_ref[0, pl.ds(qrow0, qb), :]
        kc = jnp.concatenate([k_ref[0, pl.ds(prow0, qb), :], k_ref[0, pl.ds(qrow0, qb), :]], axis=0)
        vc = jnp.concatenate([v_ref[0, pl.ds(prow0, qb), :], v_ref[0, pl.ds(qrow0, qb), :]], axis=0)
        fi = first.astype(jnp.int32)
        res = []
        for hh in range(2):
            s = lax.dot_general(q * head_mask[hh], kc, (((1,), (1,)), ((), ())),
                                preferred_element_type=F32) + bias_ref[pat, hh, fi]
            m = jnp.max(s, axis=1, keepdims=True)
            p = jnp.exp(s - m)
            l = jnp.sum(p, axis=1, keepdims=True)
            pv = jnp.dot(p.astype(BF16), vc, preferred_element_type=F32)
            res.append((pv, m, l))
        (pa, ma, la), (pb, mb, lb) = res
        return (jnp.where(is_a, pa, pb), jnp.where(is_a, ma, mb), jnp.where(is_a, la, lb))

    def merge(acc0, m0, l0, acc1, m1, l1):
        mn = jnp.maximum(m0, m1)
        a = jnp.exp(m0 - mn)
        b = jnp.exp(m1 - mn)
        return a * acc0 + b * acc1, mn, a * l0 + b * l1

    l16 = seq // 16
    l4 = seq // 4

    def body16(r, c):
        acc, m, l = unit(q16_ref, k16_ref, v16_ref, r * l16 + sb * qb, sb == 0, 2)
        rows = pl.ds(r, qb, stride=16)
        acc_s[rows, :] = acc
        m_s[rows, :] = m
        l_s[rows, :] = l
        return c

    lax.fori_loop(0, 16, body16, 0)

    def body4(j, c):
        r4 = j // 4
        mb = j % 4
        acc, m, l = unit(q4_ref, k4_ref, v4_ref, r4 * l4 + sb * (SUPER // 4) + mb * qb,
                         (sb == 0) & (mb == 0), 1)
        rows = pl.ds(mb * (4 * qb) + r4, qb, stride=4)
        acc, m, l = merge(acc_s[rows, :], m_s[rows, :], l_s[rows, :], acc, m, l)
        acc_s[rows, :] = acc
        m_s[rows, :] = m
        l_s[rows, :] = l
        return c

    lax.fori_loop(0, 16, body4, 0)

    def body1(nb, c):
        acc, m, l = unit(qn_ref, kn_ref, vn_ref, sb * SUPER + nb * qb, (sb == 0) & (nb == 0), 0)
        rows = pl.ds(pl.multiple_of(nb * qb, qb), qb)
        acc, m, l = merge(acc_s[rows, :], m_s[rows, :], l_s[rows, :], acc, m, l)
        o_ref[0, rows, :] = (acc / l).astype(BF16)
        return c

    lax.fori_loop(0, 16, body1, 0)


def _attn(slopes, qn, kn, vn, q4, k4, v4, q16, k16, v16):
    B, S, _ = qn.shape
    seq_spec = pl.BlockSpec((1, S, LANES), lambda b, p, sb, *_: (b, 0, p))
    return pl.pallas_call(
        functools.partial(_attn_kernel, seq=S),
        grid_spec=pltpu.PrefetchScalarGridSpec(
            num_scalar_prefetch=1,
            grid=(B, ATTN_W // LANES, S // SUPER),
            in_specs=[seq_spec] * 9,
            out_specs=pl.BlockSpec((1, SUPER, LANES), lambda b, p, sb, *_: (b, sb, p)),
            scratch_shapes=[pltpu.VMEM((len(PATTERNS), 2, 2, Q_BLOCK, 2 * Q_BLOCK), F32),
                            pltpu.VMEM((SUPER, LANES), F32),
                            pltpu.VMEM((SUPER, LANES), F32),
                            pltpu.VMEM((SUPER, LANES), F32)],
        ),
        out_shape=jax.ShapeDtypeStruct((B, S, ATTN_W), BF16),
        compiler_params=pltpu.CompilerParams(
            dimension_semantics=("arbitrary", "arbitrary", "arbitrary"),
            vmem_limit_bytes=VMEM_LIMIT),
        name="attn",
    )(slopes, qn, kn, vn, q4, k4, v4, q16, k16, v16)


def _ffn_kernel(x_ref, u_ref, o_ref, wout_ref, g2_ref, wup_ref, fcw_ref, fcb_ref, wdn_ref,
                out_ref, hbuf, upbuf):
    ts = ROW_TILE
    i = pl.program_id(1)
    x1 = (x_ref[0]
          + jnp.dot(u_ref[0], wout_ref[0:CONV_CH, :], preferred_element_type=F32)
          + jnp.dot(o_ref[0], wout_ref[CONV_CH:CONV_CH + ATTN_W, :], preferred_element_type=F32))
    h2 = x1 * lax.rsqrt(jnp.mean(x1 * x1, axis=-1, keepdims=True) + EPS) * g2_ref[...]

    @pl.when(i == 0)
    def _():
        hbuf[0:FFN_HALO, :] = jnp.zeros((FFN_HALO, D_MODEL), BF16)

    hbuf[FFN_HALO:FFN_HALO + ts, :] = h2.astype(BF16)
    he = hbuf[...]

    def conv_up(slot, col0):
        upbuf[slot] = jnp.dot(he, wup_ref[:, col0:col0 + FFN_COLS], preferred_element_type=F32)
        out = jnp.broadcast_to(fcb_ref[:, col0:col0 + FFN_COLS], (ts, FFN_COLS))
        for k in range(FFN_K):
            r0 = FFN_HALO - (FFN_K - 1) + k
            out = out + fcw_ref[k:k + 1, col0:col0 + FFN_COLS] * upbuf[slot, r0:r0 + ts, :]
        return out

    out_ref[0] = x1
    for c in range(D_FF // FFN_COLS):
        gate = conv_up(0, c * FFN_COLS)
        val = conv_up(1, D_FF + c * FFN_COLS)
        act = (gate * jax.nn.sigmoid(gate) * val).astype(BF16)
        out_ref[0] += jnp.dot(act, wdn_ref[c * FFN_COLS:(c + 1) * FFN_COLS, :],
                              preferred_element_type=F32)
    hbuf[0:FFN_HALO, :] = hbuf[ts:ts + FFN_HALO, :]


def _ffn(x, u, o, wout, g2, wup, fcw, fcb, wdn):
    B, S, _ = x.shape
    ts = ROW_TILE
    tok = lambda b, i: (b, i, 0)
    return pl.pallas_call(
        _ffn_kernel,
        grid=(B, S // ts),
        in_specs=[
            pl.BlockSpec((1, ts, D_MODEL), tok),
            pl.BlockSpec((1, ts, CONV_CH), tok),
            pl.BlockSpec((1, ts, ATTN_W), tok),
            _resident((CONV_CH + ATTN_W, D_MODEL)),
            _resident((1, D_MODEL)),
            _resident((D_MODEL, 2 * D_FF)),
            _resident((FFN_K, 2 * D_FF)),
            _resident((1, 2 * D_FF)),
            _resident((D_FF, D_MODEL)),
        ],
        out_specs=pl.BlockSpec((1, ts, D_MODEL), tok),
        out_shape=jax.ShapeDtypeStruct((B, S, D_MODEL), F32),
        scratch_shapes=[pltpu.VMEM((ts + FFN_HALO, D_MODEL), BF16),
                        pltpu.VMEM((2, ts + FFN_HALO, FFN_COLS), F32)],
        compiler_params=pltpu.CompilerParams(
            dimension_semantics=("arbitrary", "arbitrary"), vmem_limit_bytes=VMEM_LIMIT),
        name="ffn",
    )(x, u, o, wout, g2, wup, fcw, fcb, wdn)


def kernel(x, norm1_g, w_in, conv_w, conv_b, cn_g, cn_b, q_norm_g, k_norm_g, w_out, norm2_g,
           w_up, ffconv_w, ffconv_b, w_down):
    B, S, D = x.shape
    assert D == D_MODEL and S % SUPER == 0 and S % ROW_TILE == 0
    assert all(w // d == N_BACK for w, d in PATTERNS)
    assert D_FF % FFN_COLS == 0
    row = lambda v: v.reshape(1, -1).astype(F32)
    head = jnp.arange(ATTN_W) // HEAD_DIM
    hsum = (head[:, None] == head[None, :]).astype(BF16)
    slopes = 2.0 ** (-8.0 * jnp.arange(1, N_HEADS + 1, dtype=F32) / N_HEADS)

    u, qn, kn, vn, q4, k4, v4, q16, k16, v16 = _mix_in(
        x, row(norm1_g), w_in.astype(BF16), conv_w.astype(F32), row(conv_b), row(cn_g),
        row(cn_b), row(jnp.tile(q_norm_g, N_HEADS)), row(jnp.tile(k_norm_g, N_HEADS)), hsum)
    flat = lambda a: a.reshape(B, S, ATTN_W)
    o = _attn(slopes, qn, kn, vn, flat(q4), flat(k4), flat(v4), flat(q16), flat(k16), flat(v16))
    return _ffn(x, u, o, w_out.astype(BF16), row(norm2_g), w_up.astype(BF16),
                ffconv_w.astype(F32), row(ffconv_b), w_down.astype(BF16))
```

```python
import functools

import jax
import jax.numpy as jnp
from jax import lax
from jax.experimental import pallas as pl
from jax.experimental.pallas import tpu as pltpu

F32 = jnp.float32
BF16 = jnp.bfloat16

D_MODEL = 1024
HEAD_DIM = 64
N_HEADS = 8
CONV_CH = 512
ATTN_W = N_HEADS * HEAD_DIM
CONV_K = 31
D_FF = 2816
FFN_K = 3
EPS = 1e-6
PATTERNS = ((128, 1), (512, 4), (2048, 16))
Q_BLOCK = 128
N_BACK = 128
NEG = -1e30

LANES = 128
SUBLANES = 8
ROW_TILE = 512
CONV_HALO = 32
CONV_ROWS = 32
FFN_HALO = 16
FFN_COLS = 256
SUPER = 16 * Q_BLOCK
UNROLL = 4
VMEM_LIMIT = 52 * 1024 * 1024


def _resident(shape):
    return pl.BlockSpec(shape, lambda *_: (0,) * len(shape), pipeline_mode=pl.Buffered(1))


def _mix_in_kernel(x_ref, g1_ref, win_ref, cw_ref, cb_ref, cng_ref, cnb_ref, qg_ref, kg_ref,
                   hsum_ref,
                   u_ref, q4_ref, k4_ref, v4_ref, q16_ref, k16_ref, v16_ref,
                   gbuf, sbuf, pbuf):
    ts = ROW_TILE
    i = pl.program_id(1)
    x = x_ref[0]
    h = (x * lax.rsqrt(jnp.mean(x * x, axis=-1, keepdims=True) + EPS) * g1_ref[...]).astype(BF16)

    @pl.when(i == 0)
    def _():
        gbuf[0:CONV_HALO, :] = jnp.zeros((CONV_HALO, CONV_CH), F32)

    a = jnp.dot(h, win_ref[:, 0:2 * CONV_CH], preferred_element_type=F32)
    gbuf[CONV_HALO:CONV_HALO + ts, :] = a[:, :CONV_CH] * jax.nn.sigmoid(a[:, CONV_CH:])

    lo = SUBLANES
    for p in range(1, SUBLANES):
        sbuf[p - 1, lo:ts + CONV_HALO, :] = gbuf[lo - p:ts + CONV_HALO - p, :]

    for rc in range(ts // CONV_ROWS):
        acc = jnp.broadcast_to(cb_ref[...], (CONV_ROWS, CONV_CH))
        for k in range(CONV_K):
            blk, p = divmod(CONV_K - 1 - k, SUBLANES)
            g0 = rc * CONV_ROWS + CONV_HALO - blk * SUBLANES
            src = gbuf[g0:g0 + CONV_ROWS, :] if p == 0 else sbuf[p - 1, g0:g0 + CONV_ROWS, :]
            acc = acc + cw_ref[k:k + 1, :] * src
        mu = jnp.mean(acc, axis=-1, keepdims=True)
        d = acc - mu
        var = jnp.mean(d * d, axis=-1, keepdims=True)
        y = d * lax.rsqrt(var + EPS) * cng_ref[...] + cnb_ref[...]
        u_ref[0, rc * CONV_ROWS:(rc + 1) * CONV_ROWS, :] = (y * jax.nn.sigmoid(y)).astype(BF16)

    gbuf[0:CONV_HALO, :] = gbuf[ts:ts + CONV_HALO, :]

    def head_norm(col0, g_ref):
        z = jnp.dot(h, win_ref[:, col0:col0 + ATTN_W], preferred_element_type=F32)
        ss = jnp.dot((z * z).astype(BF16), hsum_ref[...], preferred_element_type=F32)
        return z * lax.rsqrt(ss * (1.0 / HEAD_DIM) + EPS) * g_ref[...]

    def emit(val, m4_ref, m16_ref):
        for c in range(ATTN_W // LANES):
            cols = slice(c * LANES, (c + 1) * LANES)
            pbuf[c] = val[:, cols]
            for r in range(4):
                m4_ref[0, r, :, cols] = pbuf[c, pl.ds(r, ts // 4, stride=4), :].astype(BF16)
            for r in range(16):
                m16_ref[0, r, :, cols] = pbuf[c, pl.ds(r, ts // 16, stride=16), :].astype(BF16)

    c0 = 2 * CONV_CH
    emit(head_norm(c0, qg_ref) * (HEAD_DIM ** -0.5), q4_ref, q16_ref)
    emit(head_norm(c0 + ATTN_W, kg_ref), k4_ref, k16_ref)
    emit(jnp.dot(h, win_ref[:, c0 + 2 * ATTN_W:c0 + 3 * ATTN_W], preferred_element_type=F32),
         v4_ref, v16_ref)


def _mix_in(x, g1, win, cw, cb, cng, cnb, qg, kg, hsum):
    B, S, _ = x.shape
    ts = ROW_TILE
    n_in = win.shape[1]
    tok = lambda b, i: (b, i, 0)
    strm = lambda b, i: (b, 0, i, 0)
    m4 = jax.ShapeDtypeStruct((B, 4, S // 4, ATTN_W), BF16)
    m16 = jax.ShapeDtypeStruct((B, 16, S // 16, ATTN_W), BF16)
    m4_spec = pl.BlockSpec((1, 4, ts // 4, ATTN_W), strm)
    m16_spec = pl.BlockSpec((1, 16, ts // 16, ATTN_W), strm)
    return pl.pallas_call(
        _mix_in_kernel,
        grid=(B, S // ts),
        in_specs=[
            pl.BlockSpec((1, ts, D_MODEL), tok),
            _resident((1, D_MODEL)),
            _resident((D_MODEL, n_in)),
            _resident((CONV_K, CONV_CH)),
            _resident((1, CONV_CH)),
            _resident((1, CONV_CH)),
            _resident((1, CONV_CH)),
            _resident((1, ATTN_W)),
            _resident((1, ATTN_W)),
            _resident((ATTN_W, ATTN_W)),
        ],
        out_specs=[pl.BlockSpec((1, ts, CONV_CH), tok),
                   m4_spec, m4_spec, m4_spec, m16_spec, m16_spec, m16_spec],
        out_shape=[jax.ShapeDtypeStruct((B, S, CONV_CH), BF16), m4, m4, m4, m16, m16, m16],
        scratch_shapes=[pltpu.VMEM((ts + CONV_HALO, CONV_CH), F32),
                        pltpu.VMEM((SUBLANES - 1, ts + CONV_HALO, CONV_CH), F32),
                        pltpu.VMEM((ATTN_W // LANES, ts, LANES), F32)],
        compiler_params=pltpu.CompilerParams(
            dimension_semantics=("arbitrary", "arbitrary"), vmem_limit_bytes=VMEM_LIMIT),
        name="mix_in",
    )(x, g1, win, cw, cb, cng, cnb, qg, kg, hsum)


def _attn_kernel(slopes_ref, q4_ref, k4_ref, v4_ref, q16_ref, k16_ref, v16_ref, o_ref,
                 bias_ref, acc_s, m_s, l_s, *, seq):
    pair = pl.program_id(1)
    sb = pl.program_id(2)
    qb = Q_BLOCK
    sub = qb // 4
    l4 = seq // 4
    l16 = seq // 16
    s4 = SUPER // 4
    lane = lax.broadcasted_iota(jnp.int32, (qb, LANES), 1)
    is_a = lane < HEAD_DIM
    mask_a = is_a.astype(BF16)
    mask_b = (~is_a).astype(BF16)
    ones_cols = jnp.ones((2 * qb, LANES), BF16)

    @pl.when(sb == 0)
    def _():
        row = lax.broadcasted_iota(jnp.int32, (2 * qb, 2 * qb), 0)
        col = lax.broadcasted_iota(jnp.int32, (2 * qb, 2 * qb), 1)
        qi = row & (qb - 1)
        ci = col & (qb - 1)
        slope = jnp.where(row < qb, slopes_ref[2 * pair], slopes_ref[2 * pair + 1])
        cur = col >= qb
        d_stream = qi + qb - col
        tq = 4 * (qi & (sub - 1)) + (qi >> 5)
        tk = 4 * (ci & (sub - 1)) + (ci >> 5) + jnp.where(cur, 0, -qb)
        d_tok = tq - tk
        for pat, (delta, dil) in enumerate(((d_tok, 1), (d_stream, 4), (d_stream, 16))):
            ok = (delta >= 0) & (delta <= N_BACK)
            b = jnp.where(ok, -slope * (delta * dil).astype(F32), NEG)
            bias_ref[pat, 0] = b
            bias_ref[pat, 1] = jnp.where(cur, b, NEG)

    def unit(q, kc, vc, pat, fi):
        q2 = jnp.concatenate([q * mask_a, q * mask_b], axis=0)
        s = lax.dot_general(q2, kc, (((1,), (1,)), ((), ())), preferred_element_type=F32)
        s = s + bias_ref[pat, fi]
        m = jnp.max(jnp.maximum(s[:, :qb], s[:, qb:]), axis=1, keepdims=True)
        p = jnp.exp(s - m).astype(BF16)
        pv = jnp.dot(p, jnp.concatenate([vc, ones_cols], axis=1), preferred_element_type=F32)
        acc = jnp.where(is_a, pv[:qb, :LANES], pv[qb:, :LANES])
        l = jnp.where(is_a, pv[:qb, LANES:], pv[qb:, LANES:])
        mm = jnp.where(is_a, m[:qb], m[qb:])
        return acc, mm, l

    def merge(acc0, m0, l0, acc1, m1, l1):
        mn = jnp.maximum(m0, m1)
        a = jnp.exp(m0 - mn)
        b = jnp.exp(m1 - mn)
        return a * acc0 + b * acc1, mn, a * l0 + b * l1

    def rows(ref, start, n):
        return ref[0, pl.ds(pl.multiple_of(start, 16), n), :]

    first_sb = (sb == 0).astype(jnp.int32)

    def body4(r4, c):
        base = r4 * l4 + sb * s4
        for mb in range(UNROLL):
            q0 = base + mb * qb
            p0 = jnp.maximum(q0 - qb, 0)
            kc = jnp.concatenate([rows(k4_ref, p0, qb), rows(k4_ref, q0, qb)], axis=0)
            vc = jnp.concatenate([rows(v4_ref, p0, qb), rows(v4_ref, q0, qb)], axis=0)
            acc, m, l = unit(rows(q4_ref, q0, qb), kc, vc, 1, first_sb if mb == 0 else 0)
            dst = pl.ds(pl.multiple_of(r4 * s4 + mb * qb, qb), qb)
            acc_s[dst, :] = acc
            m_s[dst, :] = m
            l_s[dst, :] = l
        return c

    lax.fori_loop(0, 4, body4, 0)

    def body16(g, c):
        for u in range(UNROLL):
            r16 = 4 * g + u
            q0 = r16 * l16 + sb * qb
            p0 = jnp.maximum(q0 - qb, 0)
            kc = jnp.concatenate([rows(k16_ref, p0, qb), rows(k16_ref, q0, qb)], axis=0)
            vc = jnp.concatenate([rows(v16_ref, p0, qb), rows(v16_ref, q0, qb)], axis=0)
            acc, m, l = unit(rows(q16_ref, q0, qb), kc, vc, 2, first_sb)
            dst = pl.ds(u * s4 + g, qb, stride=4)
            acc, m, l = merge(acc_s[dst, :], m_s[dst, :], l_s[dst, :], acc, m, l)
            acc_s[dst, :] = acc
            m_s[dst, :] = m
            l_s[dst, :] = l
        return c

    lax.fori_loop(0, 4, body16, 0)

    def body1(g, c):
        for u in range(UNROLL):
            nb = 4 * g + u
            piece = lambda ref, off, n: jnp.concatenate(
                [rows(ref, jnp.maximum(a * l4 + sb * s4 + nb * sub + off, 0), n)
                 for a in range(4)], axis=0)
            kc = jnp.concatenate([piece(k4_ref, -sub, sub), piece(k4_ref, 0, sub)], axis=0)
            vc = jnp.concatenate([piece(v4_ref, -sub, sub), piece(v4_ref, 0, sub)], axis=0)
            fi = (first_sb * (g == 0).astype(jnp.int32)) if u == 0 else 0
            acc, m, l = unit(piece(q4_ref, 0, sub), kc, vc, 0, fi)
            gather = lambda ref: jnp.concatenate(
                [ref[pl.ds(pl.multiple_of(a * s4 + nb * sub, sub), sub), :] for a in range(4)],
                axis=0)
            acc, m, l = merge(gather(acc_s), gather(m_s), gather(l_s), acc, m, l)
            o = (acc / l).astype(BF16)
            for a in range(4):
                o_ref[0, a, pl.ds(pl.multiple_of(nb * sub, sub), sub), :] = o[a * sub:(a + 1) * sub]
        return c

    lax.fori_loop(0, 4, body1, 0)


def _attn(slopes, q4, k4, v4, q16, k16, v16):
    B, S, _ = q4.shape
    seq_spec = pl.BlockSpec((1, S, LANES), lambda b, p, sb, *_: (b, 0, p))
    return pl.pallas_call(
        functools.partial(_attn_kernel, seq=S),
        grid_spec=pltpu.PrefetchScalarGridSpec(
            num_scalar_prefetch=1,
            grid=(B, ATTN_W // LANES, S // SUPER),
            in_specs=[seq_spec] * 6,
            out_specs=pl.BlockSpec((1, 4, SUPER // 4, LANES), lambda b, p, sb, *_: (b, 0, sb, p)),
            scratch_shapes=[pltpu.VMEM((len(PATTERNS), 2, 2 * Q_BLOCK, 2 * Q_BLOCK), F32),
                            pltpu.VMEM((SUPER, LANES), F32),
                            pltpu.VMEM((SUPER, LANES), F32),
                            pltpu.VMEM((SUPER, LANES), F32)],
        ),
        out_shape=jax.ShapeDtypeStruct((B, 4, S // 4, ATTN_W), BF16),
        compiler_params=pltpu.CompilerParams(
            dimension_semantics=("arbitrary", "arbitrary", "arbitrary"),
            vmem_limit_bytes=VMEM_LIMIT),
        name="attn",
    )(slopes, q4, k4, v4, q16, k16, v16)


def _ffn_kernel(x_ref, u_ref, o4_ref, wout_ref, g2_ref, wup_ref, fcw_ref, fcb_ref, wdn_ref,
                out_ref, hbuf, upbuf, obuf):
    ts = ROW_TILE
    i = pl.program_id(1)
    for c in range(ATTN_W // LANES):
        for r in range(4):
            obuf[c, pl.ds(r, ts // 4, stride=4), :] = (
                o4_ref[0, r, :, c * LANES:(c + 1) * LANES].astype(F32))
    o = jnp.concatenate([obuf[c] for c in range(ATTN_W // LANES)], axis=1).astype(BF16)
    x1 = (x_ref[0]
          + jnp.dot(u_ref[0], wout_ref[0:CONV_CH, :], preferred_element_type=F32)
          + jnp.dot(o, wout_ref[CONV_CH:CONV_CH + ATTN_W, :], preferred_element_type=F32))
    h2 = x1 * lax.rsqrt(jnp.mean(x1 * x1, axis=-1, keepdims=True) + EPS) * g2_ref[...]

    @pl.when(i == 0)
    def _():
        hbuf[0:FFN_HALO, :] = jnp.zeros((FFN_HALO, D_MODEL), BF16)

    hbuf[FFN_HALO:FFN_HALO + ts, :] = h2.astype(BF16)
    he = hbuf[...]

    def conv_up(slot, col0):
        upbuf[slot] = jnp.dot(he, wup_ref[:, col0:col0 + FFN_COLS], preferred_element_type=F32)
        out = jnp.broadcast_to(fcb_ref[:, col0:col0 + FFN_COLS], (ts, FFN_COLS))
        for k in range(FFN_K):
            r0 = FFN_HALO - (FFN_K - 1) + k
            out = out + fcw_ref[k:k + 1, col0:col0 + FFN_COLS] * upbuf[slot, r0:r0 + ts, :]
        return out

    out_ref[0] = x1
    for c in range(D_FF // FFN_COLS):
        gate = conv_up(0, c * FFN_COLS)
        val = conv_up(1, D_FF + c * FFN_COLS)
        act = (gate * jax.nn.sigmoid(gate) * val).astype(BF16)
        out_ref[0] += jnp.dot(act, wdn_ref[c * FFN_COLS:(c + 1) * FFN_COLS, :],
                              preferred_element_type=F32)
    hbuf[0:FFN_HALO, :] = hbuf[ts:ts + FFN_HALO, :]


def _ffn(x, u, o4, wout, g2, wup, fcw, fcb, wdn):
    B, S, _ = x.shape
    ts = ROW_TILE
    tok = lambda b, i: (b, i, 0)
    return pl.pallas_call(
        _ffn_kernel,
        grid=(B, S // ts),
        in_specs=[
            pl.BlockSpec((1, ts, D_MODEL), tok),
            pl.BlockSpec((1, ts, CONV_CH), tok),
            pl.BlockSpec((1, 4, ts // 4, ATTN_W), lambda b, i: (b, 0, i, 0)),
            _resident((CONV_CH + ATTN_W, D_MODEL)),
            _resident((1, D_MODEL)),
            _resident((D_MODEL, 2 * D_FF)),
            _resident((FFN_K, 2 * D_FF)),
            _resident((1, 2 * D_FF)),
            _resident((D_FF, D_MODEL)),
        ],
        out_specs=pl.BlockSpec((1, ts, D_MODEL), tok),
        out_shape=jax.ShapeDtypeStruct((B, S, D_MODEL), F32),
        scratch_shapes=[pltpu.VMEM((ts + FFN_HALO, D_MODEL), BF16),
                        pltpu.VMEM((2, ts + FFN_HALO, FFN_COLS), F32),
                        pltpu.VMEM((ATTN_W // LANES, ts, LANES), F32)],
        compiler_params=pltpu.CompilerParams(
            dimension_semantics=("arbitrary", "arbitrary"), vmem_limit_bytes=VMEM_LIMIT),
        name="ffn",
    )(x, u, o4, wout, g2, wup, fcw, fcb, wdn)


def kernel(x, norm1_g, w_in, conv_w, conv_b, cn_g, cn_b, q_norm_g, k_norm_g, w_out, norm2_g,
           w_up, ffconv_w, ffconv_b, w_down):
    B, S, D = x.shape
    assert D == D_MODEL and S % SUPER == 0 and S % ROW_TILE == 0
    assert all(w // d == N_BACK for w, d in PATTERNS) and N_BACK == Q_BLOCK
    assert D_FF % FFN_COLS == 0 and SUPER // Q_BLOCK == 4 * UNROLL
    row = lambda v: v.reshape(1, -1).astype(F32)
    head = jnp.arange(ATTN_W) // HEAD_DIM
    hsum = (head[:, None] == head[None, :]).astype(BF16)
    slopes = 2.0 ** (-8.0 * jnp.arange(1, N_HEADS + 1, dtype=F32) / N_HEADS)

    u, q4, k4, v4, q16, k16, v16 = _mix_in(
        x, row(norm1_g), w_in.astype(BF16), conv_w.astype(F32), row(conv_b), row(cn_g),
        row(cn_b), row(jnp.tile(q_norm_g, N_HEADS)), row(jnp.tile(k_norm_g, N_HEADS)), hsum)
    flat = lambda a: a.reshape(B, S, ATTN_W)
    o4 = _attn(slopes, flat(q4), flat(k4), flat(v4), flat(q16), flat(k16), flat(v16))
    return _ffn(x, u, o4, w_out.astype(BF16), row(norm2_g), w_up.astype(BF16),
                ffconv_w.astype(F32), row(ffconv_b), w_down.astype(BF16))
```

```python
import functools

import jax
import jax.numpy as jnp
from jax import lax
from jax.experimental import pallas as pl
from jax.experimental.pallas import tpu as pltpu

F32 = jnp.float32
BF16 = jnp.bfloat16

D_MODEL = 1024
HEAD_DIM = 64
N_HEADS = 8
CONV_CH = 512
ATTN_W = N_HEADS * HEAD_DIM
CONV_K = 31
D_FF = 2816
FFN_K = 3
EPS = 1e-6
PATTERNS = ((128, 1), (512, 4), (2048, 16))
Q_BLOCK = 128
N_BACK = 128
NEG = -1e30
LOG2E = 1.4426950408889634

LANES = 128
SUBLANES = 8
ROW_TILE = 512
CONV_HALO = 32
CONV_ROWS = 32
FFN_HALO = 16
FFN_COLS = 256
SUPER = 16 * Q_BLOCK
VMEM_LIMIT = 52 * 1024 * 1024


def _resident(shape):
    return pl.BlockSpec(shape, lambda *_: (0,) * len(shape), pipeline_mode=pl.Buffered(1))


def _mix_in_kernel(x_ref, g1_ref, win_ref, cw_ref, cb_ref, cng_ref, cnb_ref, qg_ref, kg_ref,
                   hsum_ref,
                   u_ref, q4_ref, k4_ref, v4_ref, q16_ref, k16_ref, v16_ref,
                   gbuf, sbuf, zbuf, pbuf4):
    ts = ROW_TILE
    i = pl.program_id(1)
    x = x_ref[0]
    h = (x * lax.rsqrt(jnp.mean(x * x, axis=-1, keepdims=True) + EPS) * g1_ref[...]).astype(BF16)

    @pl.when(i == 0)
    def _():
        gbuf[0:CONV_HALO, :] = jnp.zeros((CONV_HALO, CONV_CH), F32)

    a = jnp.dot(h, win_ref[:, 0:2 * CONV_CH], preferred_element_type=F32)
    gbuf[CONV_HALO:CONV_HALO + ts, :] = a[:, :CONV_CH] * jax.nn.sigmoid(a[:, CONV_CH:])

    lo = SUBLANES
    for p in range(1, SUBLANES):
        sbuf[p - 1, lo:ts + CONV_HALO, :] = gbuf[lo - p:ts + CONV_HALO - p, :]

    def proj(col0):
        return jnp.dot(h, win_ref[:, col0:col0 + ATTN_W], preferred_element_type=F32)

    def head_norm(z, g_ref):
        ss = jnp.dot((z * z).astype(BF16), hsum_ref[...], preferred_element_type=F32)
        return z * lax.rsqrt(ss * (1.0 / HEAD_DIM) + EPS) * g_ref[...]

    def relayout(slot, val, m4_ref, m16_ref):
        n4 = ts // 4
        n16 = ts // 16
        for c in range(ATTN_W // LANES):
            cols = slice(c * LANES, (c + 1) * LANES)
            zbuf[slot, c] = val[:, cols]
            for r in range(4):
                piece = zbuf[slot, c, pl.ds(r, n4, stride=4), :]
                m4_ref[0, r, :, cols] = piece.astype(BF16)
                pbuf4[slot, c, r * n4:(r + 1) * n4, :] = piece
            for r16 in range(16):
                piece = pbuf4[slot, c, pl.ds((r16 % 4) * n4 + r16 // 4, n16, stride=4), :]
                m16_ref[0, r16, :, cols] = piece.astype(BF16)
        return piece

    c0 = 2 * CONV_CH
    st = {}
    tasks = [
        lambda: st.__setitem__("q", proj(c0)),
        lambda: st.__setitem__("q", head_norm(st["q"], qg_ref) * (HEAD_DIM ** -0.5 * LOG2E)),
        lambda: st.__setitem__("q", relayout(0, st["q"], q4_ref, q16_ref)),
        lambda: st.__setitem__("k", proj(c0 + ATTN_W)),
        lambda: st.__setitem__("k", head_norm(st["k"], kg_ref)),
        lambda: st.__setitem__("k", relayout(1, st["k"], k4_ref, k16_ref)),
        lambda: st.__setitem__("v", proj(c0 + 2 * ATTN_W)),
        lambda: st.__setitem__("v", relayout(2, st["v"], v4_ref, v16_ref)),
    ]
    keys = ("q", "q", "q", "k", "k", "k", "v", "v")
    n_conv = ts // CONV_ROWS
    every = n_conv // len(tasks)
    assert n_conv % len(tasks) == 0

    anchor = None
    for rc in range(n_conv):
        acc = jnp.broadcast_to(cb_ref[...], (CONV_ROWS, CONV_CH))
        if anchor is not None:
            acc = acc + jnp.minimum(jnp.abs(anchor[0:1, 0:1]), 0.0)
            anchor = None
        for k in range(CONV_K):
            blk, p = divmod(CONV_K - 1 - k, SUBLANES)
            g0 = rc * CONV_ROWS + CONV_HALO - blk * SUBLANES
            src = gbuf[g0:g0 + CONV_ROWS, :] if p == 0 else sbuf[p - 1, g0:g0 + CONV_ROWS, :]
            acc = acc + cw_ref[k:k + 1, :] * src
        mu = jnp.mean(acc, axis=-1, keepdims=True)
        d = acc - mu
        var = jnp.mean(d * d, axis=-1, keepdims=True)
        y = d * lax.rsqrt(var + EPS) * cng_ref[...] + cnb_ref[...]
        u_ref[0, rc * CONV_ROWS:(rc + 1) * CONV_ROWS, :] = (y * jax.nn.sigmoid(y)).astype(BF16)
        if rc % every == 0:
            t = rc // every
            tasks[t]()
            anchor = st[keys[t]]

    gbuf[0:CONV_HALO, :] = gbuf[ts:ts + CONV_HALO, :]


def _mix_in(x, g1, win, cw, cb, cng, cnb, qg, kg, hsum):
    B, S, _ = x.shape
    ts = ROW_TILE
    n_in = win.shape[1]
    tok = lambda b, i: (b, i, 0)
    strm = lambda b, i: (b, 0, i, 0)
    m4 = jax.ShapeDtypeStruct((B, 4, S // 4, ATTN_W), BF16)
    m16 = jax.ShapeDtypeStruct((B, 16, S // 16, ATTN_W), BF16)
    m4_spec = pl.BlockSpec((1, 4, ts // 4, ATTN_W), strm)
    m16_spec = pl.BlockSpec((1, 16, ts // 16, ATTN_W), strm)
    return pl.pallas_call(
        _mix_in_kernel,
        grid=(B, S // ts),
        in_specs=[
            pl.BlockSpec((1, ts, D_MODEL), tok),
            _resident((1, D_MODEL)),
            _resident((D_MODEL, n_in)),
            _resident((CONV_K, CONV_CH)),
            _resident((1, CONV_CH)),
            _resident((1, CONV_CH)),
            _resident((1, CONV_CH)),
            _resident((1, ATTN_W)),
            _resident((1, ATTN_W)),
            _resident((ATTN_W, ATTN_W)),
        ],
        out_specs=[pl.BlockSpec((1, ts, CONV_CH), tok),
                   m4_spec, m4_spec, m4_spec, m16_spec, m16_spec, m16_spec],
        out_shape=[jax.ShapeDtypeStruct((B, S, CONV_CH), BF16), m4, m4, m4, m16, m16, m16],
        scratch_shapes=[pltpu.VMEM((ts + CONV_HALO, CONV_CH), F32),
                        pltpu.VMEM((SUBLANES - 1, ts + CONV_HALO, CONV_CH), F32),
                        pltpu.VMEM((3, ATTN_W // LANES, ts, LANES), F32),
                        pltpu.VMEM((3, ATTN_W // LANES, ts, LANES), F32)],
        compiler_params=pltpu.CompilerParams(
            dimension_semantics=("arbitrary", "arbitrary"), vmem_limit_bytes=VMEM_LIMIT),
        name="mix_in",
    )(x, g1, win, cw, cb, cng, cnb, qg, kg, hsum)


def _attn_kernel(slopes_ref, q4_ref, k4_ref, v4_ref, q16_ref, k16_ref, v16_ref, o_ref,
                 bias_ref, acc_s, m_s, l_s, *, seq):
    pair = pl.program_id(1)
    sb = pl.program_id(2)
    qb = Q_BLOCK
    sub = qb // 4
    l4 = seq // 4
    l16 = seq // 16
    s4 = SUPER // 4
    lane = lax.broadcasted_iota(jnp.int32, (qb, LANES), 1)
    is_a = lane < HEAD_DIM
    mask_a = is_a.astype(BF16)
    mask_b = (~is_a).astype(BF16)
    ones_cols = jnp.ones((2 * qb, LANES), BF16)

    @pl.when(sb == 0)
    def _():
        row = lax.broadcasted_iota(jnp.int32, (2 * qb, 2 * qb), 0)
        col = lax.broadcasted_iota(jnp.int32, (2 * qb, 2 * qb), 1)
        qi = row & (qb - 1)
        ci = col & (qb - 1)
        slope = jnp.where(row < qb, slopes_ref[2 * pair], slopes_ref[2 * pair + 1])
        cur = col >= qb
        d_stream = qi + qb - col
        tq = 4 * (qi & (sub - 1)) + (qi >> 5)
        tk = 4 * (ci & (sub - 1)) + (ci >> 5) + jnp.where(cur, 0, -qb)
        d_tok = tq - tk
        for pat, (delta, dil) in enumerate(((d_tok, 1), (d_stream, 4), (d_stream, 16))):
            ok = (delta >= 0) & (delta <= N_BACK)
            b = jnp.where(ok, -slope * (delta * dil).astype(F32) * LOG2E, NEG)
            bias_ref[pat, 0] = b
            bias_ref[pat, 1] = jnp.where(cur, b, NEG)

    def unit(q, kc, vc, pat, fi):
        q2 = jnp.concatenate([q * mask_a, q * mask_b], axis=0)
        s = lax.dot_general(q2, kc, (((1,), (1,)), ((), ())), preferred_element_type=F32)
        s = s + bias_ref[pat, fi]
        m = jnp.max(jnp.maximum(s[:, :qb], s[:, qb:]), axis=1, keepdims=True)
        p = jnp.exp2(s - m).astype(BF16)
        pv = jnp.dot(p, jnp.concatenate([vc, ones_cols], axis=1), preferred_element_type=F32)
        acc = jnp.where(is_a, pv[:qb, :LANES], pv[qb:, :LANES])
        l = jnp.where(is_a, pv[:qb, LANES:], pv[qb:, LANES:])
        mm = jnp.where(is_a, m[:qb], m[qb:])
        return acc, mm, l

    def merge(acc0, m0, l0, acc1, m1, l1):
        mn = jnp.maximum(m0, m1)
        a = jnp.exp2(m0 - mn)
        b = jnp.exp2(m1 - mn)
        return a * acc0 + b * acc1, mn, a * l0 + b * l1

    def rows(ref, start, n):
        return ref[0, pl.ds(pl.multiple_of(start, 16), n), :]

    first_sb = (sb == 0).astype(jnp.int32)
    n_units = SUPER // qb

    def block(ref, q0):
        return jnp.concatenate([rows(ref, jnp.maximum(q0 - qb, 0), qb), rows(ref, q0, qb)], axis=0)

    for j in range(n_units):
        r4, mb = divmod(j, 4)
        q0 = r4 * l4 + sb * s4 + mb * qb
        acc, m, l = unit(rows(q4_ref, q0, qb), block(k4_ref, q0), block(v4_ref, q0), 1,
                         first_sb if mb == 0 else 0)
        dst = pl.ds(r4 * s4 + mb * qb, qb)
        acc_s[dst, :] = acc
        m_s[dst, :] = m
        l_s[dst, :] = l

    for r16 in range(n_units):
        q0 = r16 * l16 + sb * qb
        acc, m, l = unit(rows(q16_ref, q0, qb), block(k16_ref, q0), block(v16_ref, q0), 2,
                         first_sb)
        dst = pl.ds((r16 % 4) * s4 + r16 // 4, qb, stride=4)
        acc, m, l = merge(acc_s[dst, :], m_s[dst, :], l_s[dst, :], acc, m, l)
        acc_s[dst, :] = acc
        m_s[dst, :] = m
        l_s[dst, :] = l

    for nb in range(n_units):
        piece = lambda ref, off: jnp.concatenate(
            [rows(ref, jnp.maximum(a * l4 + sb * s4 + nb * sub + off, 0), sub) for a in range(4)],
            axis=0)
        kc = jnp.concatenate([piece(k4_ref, -sub), piece(k4_ref, 0)], axis=0)
        vc = jnp.concatenate([piece(v4_ref, -sub), piece(v4_ref, 0)], axis=0)
        acc, m, l = unit(piece(q4_ref, 0), kc, vc, 0, first_sb if nb == 0 else 0)
        gather = lambda ref: jnp.concatenate(
            [ref[a * s4 + nb * sub:a * s4 + (nb + 1) * sub, :] for a in range(4)], axis=0)
        acc, m, l = merge(gather(acc_s), gather(m_s), gather(l_s), acc, m, l)
        o = (acc / l).astype(BF16)
        for a in range(4):
            o_ref[0, a, nb * sub:(nb + 1) * sub, :] = o[a * sub:(a + 1) * sub]


def _attn(slopes, q4, k4, v4, q16, k16, v16):
    B, S, _ = q4.shape
    seq_spec = pl.BlockSpec((1, S, LANES), lambda b, p, sb, *_: (b, 0, p))
    return pl.pallas_call(
        functools.partial(_attn_kernel, seq=S),
        grid_spec=pltpu.PrefetchScalarGridSpec(
            num_scalar_prefetch=1,
            grid=(B, ATTN_W // LANES, S // SUPER),
            in_specs=[seq_spec] * 6,
            out_specs=pl.BlockSpec((1, 4, SUPER // 4, LANES), lambda b, p, sb, *_: (b, 0, sb, p)),
            scratch_shapes=[pltpu.VMEM((len(PATTERNS), 2, 2 * Q_BLOCK, 2 * Q_BLOCK), F32),
                            pltpu.VMEM((SUPER, LANES), F32),
                            pltpu.VMEM((SUPER, LANES), F32),
                            pltpu.VMEM((SUPER, LANES), F32)],
        ),
        out_shape=jax.ShapeDtypeStruct((B, 4, S // 4, ATTN_W), BF16),
        compiler_params=pltpu.CompilerParams(
            dimension_semantics=("arbitrary", "arbitrary", "arbitrary"),
            vmem_limit_bytes=VMEM_LIMIT),
        name="attn",
    )(slopes, q4, k4, v4, q16, k16, v16)


def _ffn_kernel(x_ref, u_ref, o4_ref, wout_ref, g2_ref, wup_ref, fcw_ref, fcb_ref, wdn_ref,
                out_ref, hbuf, upbuf, obuf, actbuf):
    ts = ROW_TILE
    i = pl.program_id(1)
    for c in range(ATTN_W // LANES):
        for r in range(4):
            obuf[c, pl.ds(r, ts // 4, stride=4), :] = (
                o4_ref[0, r, :, c * LANES:(c + 1) * LANES].astype(F32))
    o = jnp.concatenate([obuf[c] for c in range(ATTN_W // LANES)], axis=1).astype(BF16)
    x1 = (x_ref[0]
          + jnp.dot(u_ref[0], wout_ref[0:CONV_CH, :], preferred_element_type=F32)
          + jnp.dot(o, wout_ref[CONV_CH:CONV_CH + ATTN_W, :], preferred_element_type=F32))
    h2 = x1 * lax.rsqrt(jnp.mean(x1 * x1, axis=-1, keepdims=True) + EPS) * g2_ref[...]

    @pl.when(i == 0)
    def _():
        hbuf[0:FFN_HALO, :] = jnp.zeros((FFN_HALO, D_MODEL), BF16)

    hbuf[FFN_HALO:FFN_HALO + ts, :] = h2.astype(BF16)
    he = hbuf[...]

    def up_proj(c):
        for j, col0 in enumerate((c * FFN_COLS, D_FF + c * FFN_COLS)):
            upbuf[2 * (c % 2) + j] = jnp.dot(he, wup_ref[:, col0:col0 + FFN_COLS],
                                             preferred_element_type=F32)

    def conv(slot, col0):
        out = jnp.broadcast_to(fcb_ref[:, col0:col0 + FFN_COLS], (ts, FFN_COLS))
        for k in range(FFN_K):
            r0 = FFN_HALO - (FFN_K - 1) + k
            out = out + fcw_ref[k:k + 1, col0:col0 + FFN_COLS] * upbuf[slot, r0:r0 + ts, :]
        return out

    def gate_mul(c):
        gate = conv(2 * (c % 2), c * FFN_COLS)
        val = conv(2 * (c % 2) + 1, D_FF + c * FFN_COLS)
        actbuf[c % 2] = (gate * jax.nn.sigmoid(gate) * val).astype(BF16)

    def down_proj(c):
        out_ref[0] += jnp.dot(actbuf[c % 2], wdn_ref[c * FFN_COLS:(c + 1) * FFN_COLS, :],
                              preferred_element_type=F32)

    out_ref[0] = x1
    n_chunks = D_FF // FFN_COLS
    up_proj(0)
    for c in range(n_chunks):
        if c + 1 < n_chunks:
            up_proj(c + 1)
        gate_mul(c)
        if c > 0:
            down_proj(c - 1)
    down_proj(n_chunks - 1)
    hbuf[0:FFN_HALO, :] = hbuf[ts:ts + FFN_HALO, :]


def _ffn(x, u, o4, wout, g2, wup, fcw, fcb, wdn):
    B, S, _ = x.shape
    ts = ROW_TILE
    tok = lambda b, i: (b, i, 0)
    return pl.pallas_call(
        _ffn_kernel,
        grid=(B, S // ts),
        in_specs=[
            pl.BlockSpec((1, ts, D_MODEL), tok),
            pl.BlockSpec((1, ts, CONV_CH), tok),
            pl.BlockSpec((1, 4, ts // 4, ATTN_W), lambda b, i: (b, 0, i, 0)),
            _resident((CONV_CH + ATTN_W, D_MODEL)),
            _resident((1, D_MODEL)),
            _resident((D_MODEL, 2 * D_FF)),
            _resident((FFN_K, 2 * D_FF)),
            _resident((1, 2 * D_FF)),
            _resident((D_FF, D_MODEL)),
        ],
        out_specs=pl.BlockSpec((1, ts, D_MODEL), tok),
        out_shape=jax.ShapeDtypeStruct((B, S, D_MODEL), F32),
        scratch_shapes=[pltpu.VMEM((ts + FFN_HALO, D_MODEL), BF16),
                        pltpu.VMEM((4, ts + FFN_HALO, FFN_COLS), F32),
                        pltpu.VMEM((ATTN_W // LANES, ts, LANES), F32),
                        pltpu.VMEM((2, ts, FFN_COLS), BF16)],
        compiler_params=pltpu.CompilerParams(
            dimension_semantics=("arbitrary", "arbitrary"), vmem_limit_bytes=VMEM_LIMIT),
        name="ffn",
    )(x, u, o4, wout, g2, wup, fcw, fcb, wdn)


def kernel(x, norm1_g, w_in, conv_w, conv_b, cn_g, cn_b, q_norm_g, k_norm_g, w_out, norm2_g,
           w_up, ffconv_w, ffconv_b, w_down):
    B, S, D = x.shape
    assert D == D_MODEL and S % SUPER == 0 and S % ROW_TILE == 0
    assert all(w // d == N_BACK for w, d in PATTERNS) and N_BACK == Q_BLOCK
    assert D_FF % FFN_COLS == 0
    row = lambda v: v.reshape(1, -1).astype(F32)
    head = jnp.arange(ATTN_W) // HEAD_DIM
    hsum = (head[:, None] == head[None, :]).astype(BF16)
    slopes = 2.0 ** (-8.0 * jnp.arange(1, N_HEADS + 1, dtype=F32) / N_HEADS)

    u, q4, k4, v4, q16, k16, v16 = _mix_in(
        x, row(norm1_g), w_in.astype(BF16), conv_w.astype(F32), row(conv_b), row(cn_g),
        row(cn_b), row(jnp.tile(q_norm_g, N_HEADS)), row(jnp.tile(k_norm_g, N_HEADS)), hsum)
    flat = lambda a: a.reshape(B, S, ATTN_W)
    o4 = _attn(slopes, flat(q4), flat(k4), flat(v4), flat(q16), flat(k16), flat(v16))
    return _ffn(x, u, o4, w_out.astype(BF16), row(norm2_g), w_up.astype(BF16),
                ffconv_w.astype(F32), row(ffconv_b), w_down.astype(BF16))
```

```python
import functools

import jax
import jax.numpy as jnp
from jax import lax
from jax.experimental import pallas as pl
from jax.experimental.pallas import tpu as pltpu

F32 = jnp.float32
BF16 = jnp.bfloat16

D_MODEL = 1024
HEAD_DIM = 64
N_HEADS = 8
CONV_CH = 512
ATTN_W = N_HEADS * HEAD_DIM
CONV_K = 31
D_FF = 2816
FFN_K = 3
EPS = 1e-6
PATTERNS = ((128, 1), (512, 4), (2048, 16))
Q_BLOCK = 128
N_BACK = 128
NEG = -1e30
LOG2E = 1.4426950408889634

LANES = 128
SUBLANES = 8
ROW_TILE = 512
CONV_HALO = 32
CONV_ROWS = 64
FFN_HALO = 16
FFN_COLS = 256
DOWN_CHUNKS = 2
FRONT_BLOCKS = 4
SUPER = 16 * Q_BLOCK
VMEM_LIMIT = 52 * 1024 * 1024


def _resident(shape):
    return pl.BlockSpec(shape, lambda *_: (0,) * len(shape), pipeline_mode=pl.Buffered(1))


def _mix_in_kernel(x_ref, g1_ref, win_ref, cw_ref, cb_ref, cng_ref, cnb_ref, qg_ref, kg_ref,
                   hsum_ref,
                   u_ref, q4_ref, k4_ref, v4_ref, q16_ref, k16_ref, v16_ref,
                   gbuf, sbuf, zbuf, pbuf4):
    ts = ROW_TILE
    i = pl.program_id(1)

    @pl.when(i == 0)
    def _():
        gbuf[0:CONV_HALO, :] = jnp.zeros((CONV_HALO, CONV_CH), F32)

    blk = ts // FRONT_BLOCKS
    hs = []
    for r in range(FRONT_BLOCKS):
        x = x_ref[0, r * blk:(r + 1) * blk, :]
        hr = (x * lax.rsqrt(jnp.mean(x * x, axis=-1, keepdims=True) + EPS) * g1_ref[...]).astype(BF16)
        a = jnp.dot(hr, win_ref[:, 0:2 * CONV_CH], preferred_element_type=F32)
        gbuf[CONV_HALO + r * blk:CONV_HALO + (r + 1) * blk, :] = (
            a[:, :CONV_CH] * jax.nn.sigmoid(a[:, CONV_CH:]))
        hs.append(hr)
    h = jnp.concatenate(hs, axis=0)

    lo = SUBLANES
    for p in range(1, SUBLANES):
        sbuf[p - 1, lo:ts + CONV_HALO, :] = gbuf[lo - p:ts + CONV_HALO - p, :]

    def proj(col0):
        return jnp.dot(h, win_ref[:, col0:col0 + ATTN_W], preferred_element_type=F32)

    def head_norm(z, g_ref):
        ss = jnp.dot((z * z).astype(BF16), hsum_ref[...], preferred_element_type=F32)
        return z * lax.rsqrt(ss * (1.0 / HEAD_DIM) + EPS) * g_ref[...]

    def relayout(slot, val, m4_ref, m16_ref):
        n4 = ts // 4
        n16 = ts // 16
        for c in range(ATTN_W // LANES):
            cols = slice(c * LANES, (c + 1) * LANES)
            zbuf[slot, c] = val[:, cols]
            for r in range(4):
                piece = zbuf[slot, c, pl.ds(r, n4, stride=4), :]
                m4_ref[0, r, :, cols] = piece.astype(BF16)
                pbuf4[slot, c, r * n4:(r + 1) * n4, :] = piece
            for r16 in range(16):
                piece = pbuf4[slot, c, pl.ds((r16 % 4) * n4 + r16 // 4, n16, stride=4), :]
                m16_ref[0, r16, :, cols] = piece.astype(BF16)
        return piece

    c0 = 2 * CONV_CH
    st = {}
    tasks = [
        lambda: st.__setitem__("q", proj(c0)),
        lambda: st.__setitem__("q", head_norm(st["q"], qg_ref) * (HEAD_DIM ** -0.5 * LOG2E)),
        lambda: st.__setitem__("q", relayout(0, st["q"], q4_ref, q16_ref)),
        lambda: st.__setitem__("k", proj(c0 + ATTN_W)),
        lambda: st.__setitem__("k", head_norm(st["k"], kg_ref)),
        lambda: st.__setitem__("k", relayout(1, st["k"], k4_ref, k16_ref)),
        lambda: st.__setitem__("v", proj(c0 + 2 * ATTN_W)),
        lambda: st.__setitem__("v", relayout(2, st["v"], v4_ref, v16_ref)),
    ]
    keys = ("q", "q", "q", "k", "k", "k", "v", "v")
    n_conv = ts // CONV_ROWS
    every = n_conv // len(tasks)
    assert n_conv % len(tasks) == 0

    anchor = None
    for rc in range(n_conv):
        acc = jnp.broadcast_to(cb_ref[...], (CONV_ROWS, CONV_CH))
        if anchor is not None:
            acc = acc + jnp.minimum(jnp.abs(anchor[0:1, 0:1]), 0.0)
            anchor = None
        for k in range(CONV_K):
            blk, p = divmod(CONV_K - 1 - k, SUBLANES)
            g0 = rc * CONV_ROWS + CONV_HALO - blk * SUBLANES
            src = gbuf[g0:g0 + CONV_ROWS, :] if p == 0 else sbuf[p - 1, g0:g0 + CONV_ROWS, :]
            acc = acc + cw_ref[k:k + 1, :] * src
        mu = jnp.mean(acc, axis=-1, keepdims=True)
        d = acc - mu
        var = jnp.mean(d * d, axis=-1, keepdims=True)
        y = d * lax.rsqrt(var + EPS) * cng_ref[...] + cnb_ref[...]
        u_ref[0, rc * CONV_ROWS:(rc + 1) * CONV_ROWS, :] = (y * jax.nn.sigmoid(y)).astype(BF16)
        if rc % every == 0:
            t = rc // every
            tasks[t]()
            anchor = st[keys[t]]

    gbuf[0:CONV_HALO, :] = gbuf[ts:ts + CONV_HALO, :]


def _mix_in(x, g1, win, cw, cb, cng, cnb, qg, kg, hsum):
    B, S, _ = x.shape
    ts = ROW_TILE
    n_in = win.shape[1]
    tok = lambda b, i: (b, i, 0)
    strm = lambda b, i: (b, 0, i, 0)
    m4 = jax.ShapeDtypeStruct((B, 4, S // 4, ATTN_W), BF16)
    m16 = jax.ShapeDtypeStruct((B, 16, S // 16, ATTN_W), BF16)
    m4_spec = pl.BlockSpec((1, 4, ts // 4, ATTN_W), strm)
    m16_spec = pl.BlockSpec((1, 16, ts // 16, ATTN_W), strm)
    return pl.pallas_call(
        _mix_in_kernel,
        grid=(B, S // ts),
        in_specs=[
            pl.BlockSpec((1, ts, D_MODEL), tok),
            _resident((1, D_MODEL)),
            _resident((D_MODEL, n_in)),
            _resident((CONV_K, CONV_CH)),
            _resident((1, CONV_CH)),
            _resident((1, CONV_CH)),
            _resident((1, CONV_CH)),
            _resident((1, ATTN_W)),
            _resident((1, ATTN_W)),
            _resident((ATTN_W, ATTN_W)),
        ],
        out_specs=[pl.BlockSpec((1, ts, CONV_CH), tok),
                   m4_spec, m4_spec, m4_spec, m16_spec, m16_spec, m16_spec],
        out_shape=[jax.ShapeDtypeStruct((B, S, CONV_CH), BF16), m4, m4, m4, m16, m16, m16],
        scratch_shapes=[pltpu.VMEM((ts + CONV_HALO, CONV_CH), F32),
                        pltpu.VMEM((SUBLANES - 1, ts + CONV_HALO, CONV_CH), F32),
                        pltpu.VMEM((3, ATTN_W // LANES, ts, LANES), F32),
                        pltpu.VMEM((3, ATTN_W // LANES, ts, LANES), F32)],
        compiler_params=pltpu.CompilerParams(
            dimension_semantics=("arbitrary", "arbitrary"), vmem_limit_bytes=VMEM_LIMIT),
        name="mix_in",
    )(x, g1, win, cw, cb, cng, cnb, qg, kg, hsum)


def _attn_kernel(slopes_ref, q4_ref, k4_ref, v4_ref, q16_ref, k16_ref, v16_ref, o_ref,
                 bias_ref, acc_s, m_s, l_s, *, seq):
    pair = pl.program_id(1)
    sb = pl.program_id(2)
    qb = Q_BLOCK
    sub = qb // 4
    l4 = seq // 4
    l16 = seq // 16
    s4 = SUPER // 4
    lane = lax.broadcasted_iota(jnp.int32, (qb, LANES), 1)
    is_a = lane < HEAD_DIM
    mask_a = is_a.astype(BF16)
    mask_b = (~is_a).astype(BF16)
    ones_cols = jnp.ones((2 * qb, LANES), BF16)

    @pl.when(sb == 0)
    def _():
        row = lax.broadcasted_iota(jnp.int32, (2 * qb, 2 * qb), 0)
        col = lax.broadcasted_iota(jnp.int32, (2 * qb, 2 * qb), 1)
        qi = row & (qb - 1)
        ci = col & (qb - 1)
        slope = jnp.where(row < qb, slopes_ref[2 * pair], slopes_ref[2 * pair + 1])
        cur = col >= qb
        d_stream = qi + qb - col
        tq = 4 * (qi & (sub - 1)) + (qi >> 5)
        tk = 4 * (ci & (sub - 1)) + (ci >> 5) + jnp.where(cur, 0, -qb)
        d_tok = tq - tk
        for pat, (delta, dil) in enumerate(((d_tok, 1), (d_stream, 4), (d_stream, 16))):
            ok = (delta >= 0) & (delta <= N_BACK)
            b = jnp.where(ok, -slope * (delta * dil).astype(F32) * LOG2E, NEG)
            bias_ref[pat, 0] = b
            bias_ref[pat, 1] = jnp.where(cur, b, NEG)

    def unit(q, kc, vc, pat, fi):
        q2 = jnp.concatenate([q * mask_a, q * mask_b], axis=0)
        s = lax.dot_general(q2, kc, (((1,), (1,)), ((), ())), preferred_element_type=F32)
        s = s + bias_ref[pat, fi]
        m = jnp.max(jnp.maximum(s[:, :qb], s[:, qb:]), axis=1, keepdims=True)
        p = jnp.exp2(s - m).astype(BF16)
        pv = jnp.dot(p, jnp.concatenate([vc, ones_cols], axis=1), preferred_element_type=F32)
        acc = jnp.where(is_a, pv[:qb, :LANES], pv[qb:, :LANES])
        l = jnp.where(is_a, pv[:qb, LANES:], pv[qb:, LANES:])
        mm = jnp.where(is_a, m[:qb], m[qb:])
        return acc, mm, l

    def merge(acc0, m0, l0, acc1, m1, l1):
        mn = jnp.maximum(m0, m1)
        a = jnp.exp2(m0 - mn)
        b = jnp.exp2(m1 - mn)
        return a * acc0 + b * acc1, mn, a * l0 + b * l1

    def rows(ref, start, n):
        return ref[0, pl.ds(pl.multiple_of(start, 16), n), :]

    first_sb = (sb == 0).astype(jnp.int32)
    n_units = SUPER // qb

    def block(ref, q0):
        return jnp.concatenate([rows(ref, jnp.maximum(q0 - qb, 0), qb), rows(ref, q0, qb)], axis=0)

    for j in range(n_units):
        r4, mb = divmod(j, 4)
        q0 = r4 * l4 + sb * s4 + mb * qb
        acc, m, l = unit(rows(q4_ref, q0, qb), block(k4_ref, q0), block(v4_ref, q0), 1,
                         first_sb if mb == 0 else 0)
        dst = pl.ds(r4 * s4 + mb * qb, qb)
        acc_s[dst, :] = acc
        m_s[dst, :] = m
        l_s[dst, :] = l

    for r16 in range(n_units):
        q0 = r16 * l16 + sb * qb
        acc, m, l = unit(rows(q16_ref, q0, qb), block(k16_ref, q0), block(v16_ref, q0), 2,
                         first_sb)
        dst = pl.ds((r16 % 4) * s4 + r16 // 4, qb, stride=4)
        acc, m, l = merge(acc_s[dst, :], m_s[dst, :], l_s[dst, :], acc, m, l)
        acc_s[dst, :] = acc
        m_s[dst, :] = m
        l_s[dst, :] = l

    for nb in range(n_units):
        piece = lambda ref, off: jnp.concatenate(
            [rows(ref, jnp.maximum(a * l4 + sb * s4 + nb * sub + off, 0), sub) for a in range(4)],
            axis=0)
        kc = jnp.concatenate([piece(k4_ref, -sub), piece(k4_ref, 0)], axis=0)
        vc = jnp.concatenate([piece(v4_ref, -sub), piece(v4_ref, 0)], axis=0)
        acc, m, l = unit(piece(q4_ref, 0), kc, vc, 0, first_sb if nb == 0 else 0)
        gather = lambda ref: jnp.concatenate(
            [ref[a * s4 + nb * sub:a * s4 + (nb + 1) * sub, :] for a in range(4)], axis=0)
        acc, m, l = merge(gather(acc_s), gather(m_s), gather(l_s), acc, m, l)
        o = (acc / l).astype(BF16)
        for a in range(4):
            o_ref[0, a, nb * sub:(nb + 1) * sub, :] = o[a * sub:(a + 1) * sub]


def _attn(slopes, q4, k4, v4, q16, k16, v16):
    B, S, _ = q4.shape
    seq_spec = pl.BlockSpec((1, S, LANES), lambda b, p, sb, *_: (b, 0, p))
    return pl.pallas_call(
        functools.partial(_attn_kernel, seq=S),
        grid_spec=pltpu.PrefetchScalarGridSpec(
            num_scalar_prefetch=1,
            grid=(B, ATTN_W // LANES, S // SUPER),
            in_specs=[seq_spec] * 6,
            out_specs=pl.BlockSpec((1, 4, SUPER // 4, LANES), lambda b, p, sb, *_: (b, 0, sb, p)),
            scratch_shapes=[pltpu.VMEM((len(PATTERNS), 2, 2 * Q_BLOCK, 2 * Q_BLOCK), F32),
                            pltpu.VMEM((SUPER, LANES), F32),
                            pltpu.VMEM((SUPER, LANES), F32),
                            pltpu.VMEM((SUPER, LANES), F32)],
        ),
        out_shape=jax.ShapeDtypeStruct((B, 4, S // 4, ATTN_W), BF16),
        compiler_params=pltpu.CompilerParams(
            dimension_semantics=("arbitrary", "arbitrary", "arbitrary"),
            vmem_limit_bytes=VMEM_LIMIT),
        name="attn",
    )(slopes, q4, k4, v4, q16, k16, v16)


def _ffn_kernel(x_ref, u_ref, o4_ref, wout_ref, g2_ref, wup_ref, fcw_ref, fcb_ref, wdn_ref,
                out_ref, hbuf, upbuf, obuf, actbuf):
    ts = ROW_TILE
    i = pl.program_id(1)
    for c in range(ATTN_W // LANES):
        for r in range(4):
            obuf[c, pl.ds(r, ts // 4, stride=4), :] = (
                o4_ref[0, r, :, c * LANES:(c + 1) * LANES].astype(F32))
    @pl.when(i == 0)
    def _():
        hbuf[0:FFN_HALO, :] = jnp.zeros((FFN_HALO, D_MODEL), BF16)

    blk = ts // FRONT_BLOCKS
    for r in range(FRONT_BLOCKS):
        rs = slice(r * blk, (r + 1) * blk)
        o = jnp.concatenate([obuf[c, rs, :] for c in range(ATTN_W // LANES)], axis=1).astype(BF16)
        x1 = (x_ref[0, rs, :]
              + jnp.dot(u_ref[0, rs, :], wout_ref[0:CONV_CH, :], preferred_element_type=F32)
              + jnp.dot(o, wout_ref[CONV_CH:CONV_CH + ATTN_W, :], preferred_element_type=F32))
        out_ref[0, rs, :] = x1
        h2 = x1 * lax.rsqrt(jnp.mean(x1 * x1, axis=-1, keepdims=True) + EPS) * g2_ref[...]
        hbuf[FFN_HALO + r * blk:FFN_HALO + (r + 1) * blk, :] = h2.astype(BF16)

    he = hbuf[...]

    def up_proj(c):
        for j, col0 in enumerate((c * FFN_COLS, D_FF + c * FFN_COLS)):
            upbuf[2 * (c % 2) + j] = jnp.dot(he, wup_ref[:, col0:col0 + FFN_COLS],
                                             preferred_element_type=F32)

    def conv(slot, col0):
        out = jnp.broadcast_to(fcb_ref[:, col0:col0 + FFN_COLS], (ts, FFN_COLS))
        for k in range(FFN_K):
            r0 = FFN_HALO - (FFN_K - 1) + k
            out = out + fcw_ref[k:k + 1, col0:col0 + FFN_COLS] * upbuf[slot, r0:r0 + ts, :]
        return out

    def gate_mul(c):
        gate = conv(2 * (c % 2), c * FFN_COLS)
        val = conv(2 * (c % 2) + 1, D_FF + c * FFN_COLS)
        actbuf[:, c * FFN_COLS:(c + 1) * FFN_COLS] = (gate * jax.nn.sigmoid(gate) * val).astype(BF16)

    def down_proj(c_lo, c_hi):
        out_ref[0] += jnp.dot(actbuf[:, c_lo * FFN_COLS:c_hi * FFN_COLS],
                              wdn_ref[c_lo * FFN_COLS:c_hi * FFN_COLS, :],
                              preferred_element_type=F32)

    n_chunks = D_FF // FFN_COLS
    up_proj(0)
    pending = 0
    for c in range(n_chunks):
        if c + 1 < n_chunks:
            up_proj(c + 1)
        gate_mul(c)
        if c - pending >= DOWN_CHUNKS:
            down_proj(pending, pending + DOWN_CHUNKS)
            pending += DOWN_CHUNKS
    down_proj(pending, n_chunks)
    hbuf[0:FFN_HALO, :] = hbuf[ts:ts + FFN_HALO, :]


def _ffn(x, u, o4, wout, g2, wup, fcw, fcb, wdn):
    B, S, _ = x.shape
    ts = ROW_TILE
    tok = lambda b, i: (b, i, 0)
    return pl.pallas_call(
        _ffn_kernel,
        grid=(B, S // ts),
        in_specs=[
            pl.BlockSpec((1, ts, D_MODEL), tok),
            pl.BlockSpec((1, ts, CONV_CH), tok),
            pl.BlockSpec((1, 4, ts // 4, ATTN_W), lambda b, i: (b, 0, i, 0)),
            _resident((CONV_CH + ATTN_W, D_MODEL)),
            _resident((1, D_MODEL)),
            _resident((D_MODEL, 2 * D_FF)),
            _resident((FFN_K, 2 * D_FF)),
            _resident((1, 2 * D_FF)),
            _resident((D_FF, D_MODEL)),
        ],
        out_specs=pl.BlockSpec((1, ts, D_MODEL), tok),
        out_shape=jax.ShapeDtypeStruct((B, S, D_MODEL), F32),
        scratch_shapes=[pltpu.VMEM((ts + FFN_HALO, D_MODEL), BF16),
                        pltpu.VMEM((4, ts + FFN_HALO, FFN_COLS), F32),
                        pltpu.VMEM((ATTN_W // LANES, ts, LANES), F32),
                        pltpu.VMEM((ts, D_FF), BF16)],
        compiler_params=pltpu.CompilerParams(
            dimension_semantics=("arbitrary", "arbitrary"), vmem_limit_bytes=VMEM_LIMIT),
        name="ffn",
    )(x, u, o4, wout, g2, wup, fcw, fcb, wdn)


def kernel(x, norm1_g, w_in, conv_w, conv_b, cn_g, cn_b, q_norm_g, k_norm_g, w_out, norm2_g,
           w_up, ffconv_w, ffconv_b, w_down):
    B, S, D = x.shape
    assert D == D_MODEL and S % SUPER == 0 and S % ROW_TILE == 0
    assert all(w // d == N_BACK for w, d in PATTERNS) and N_BACK == Q_BLOCK
    assert D_FF % FFN_COLS == 0
    row = lambda v: v.reshape(1, -1).astype(F32)
    head = jnp.arange(ATTN_W) // HEAD_DIM
    hsum = (head[:, None] == head[None, :]).astype(BF16)
    slopes = 2.0 ** (-8.0 * jnp.arange(1, N_HEADS + 1, dtype=F32) / N_HEADS)

    u, q4, k4, v4, q16, k16, v16 = _mix_in(
        x, row(norm1_g), w_in.astype(BF16), conv_w.astype(F32), row(conv_b), row(cn_g),
        row(cn_b), row(jnp.tile(q_norm_g, N_HEADS)), row(jnp.tile(k_norm_g, N_HEADS)), hsum)
    flat = lambda a: a.reshape(B, S, ATTN_W)
    o4 = _attn(slopes, flat(q4), flat(k4), flat(v4), flat(q16), flat(k16), flat(v16))
    return _ffn(x, u, o4, w_out.astype(BF16), row(norm2_g), w_up.astype(BF16),
                ffconv_w.astype(F32), row(ffconv_b), w_down.astype(BF16))
```

```python
import functools

import jax
import jax.numpy as jnp
from jax import lax
from jax.experimental import pallas as pl
from jax.experimental.pallas import tpu as pltpu

F32 = jnp.float32
BF16 = jnp.bfloat16

D_MODEL = 1024
HEAD_DIM = 64
N_HEADS = 8
CONV_CH = 512
ATTN_W = N_HEADS * HEAD_DIM
CONV_K = 31
D_FF = 2816
FFN_K = 3
EPS = 1e-6
PATTERNS = ((128, 1), (512, 4), (2048, 16))
Q_BLOCK = 128
N_BACK = 128
NEG = -1e30
LOG2E = 1.4426950408889634

LANES = 128
SUBLANES = 8
BF16_ROWS = 16
ROW_TILE = 512
CONV_HALO = 32
CONV_ROWS = 64
CONV_LANES = 128
FFN_HALO = 16
FFN_COLS = 256
DOWN_CHUNKS = 2
FRONT_BLOCKS = 4
SUPER = 16 * Q_BLOCK
VMEM_LIMIT = 52 * 1024 * 1024


def _resident(shape):
    return pl.BlockSpec(shape, lambda *_: (0,) * len(shape), pipeline_mode=pl.Buffered(1))


def _mix_in_kernel(x_ref, g1_ref, win_ref, cw_ref, cb_ref, cng_ref, cnb_ref, qg_ref, kg_ref,
                   hsum_ref,
                   u_ref, q4_ref, k4_ref, v4_ref, q16_ref, k16_ref, v16_ref,
                   gbuf, sbuf, zbuf, pbuf4):
    ts = ROW_TILE
    i = pl.program_id(1)

    @pl.when(i == 0)
    def _():
        gbuf[0:CONV_HALO, :] = jnp.zeros((CONV_HALO, CONV_CH), F32)

    blk = ts // FRONT_BLOCKS
    hs = []
    for r in range(FRONT_BLOCKS):
        x = x_ref[0, r * blk:(r + 1) * blk, :]
        hr = (x * lax.rsqrt(jnp.mean(x * x, axis=-1, keepdims=True) + EPS) * g1_ref[...]).astype(BF16)
        a = jnp.dot(hr, win_ref[:, 0:2 * CONV_CH], preferred_element_type=F32)
        gbuf[CONV_HALO + r * blk:CONV_HALO + (r + 1) * blk, :] = (
            a[:, :CONV_CH] * jax.nn.sigmoid(a[:, CONV_CH:]))
        hs.append(hr)
    h = jnp.concatenate(hs, axis=0)

    lo = SUBLANES
    for p in range(1, SUBLANES):
        sbuf[p - 1, lo:ts + CONV_HALO, :] = gbuf[lo - p:ts + CONV_HALO - p, :]

    def proj(col0):
        return jnp.dot(h, win_ref[:, col0:col0 + ATTN_W], preferred_element_type=F32)

    def head_norm(z, g_ref):
        ss = jnp.dot((z * z).astype(BF16), hsum_ref[...], preferred_element_type=F32)
        return z * lax.rsqrt(ss * (1.0 / HEAD_DIM) + EPS) * g_ref[...]

    def relayout(slot, val, m4_ref, m16_ref):
        n4 = ts // 4
        n16 = ts // 16
        for c in range(ATTN_W // LANES):
            cols = slice(c * LANES, (c + 1) * LANES)
            zbuf[slot, c] = val[:, cols]
            for r in range(4):
                piece = zbuf[slot, c, pl.ds(r, n4, stride=4), :]
                m4_ref[0, r, :, cols] = piece.astype(BF16)
                pbuf4[slot, c, r * n4:(r + 1) * n4, :] = piece
            for r16 in range(16):
                piece = pbuf4[slot, c, pl.ds((r16 % 4) * n4 + r16 // 4, n16, stride=4), :]
                m16_ref[0, r16, :, cols] = piece.astype(BF16)
        return piece

    c0 = 2 * CONV_CH
    st = {}
    tasks = [
        lambda: st.__setitem__("q", proj(c0)),
        lambda: st.__setitem__("q", head_norm(st["q"], qg_ref) * (HEAD_DIM ** -0.5 * LOG2E)),
        lambda: st.__setitem__("q", relayout(0, st["q"], q4_ref, q16_ref)),
        lambda: st.__setitem__("k", proj(c0 + ATTN_W)),
        lambda: st.__setitem__("k", head_norm(st["k"], kg_ref)),
        lambda: st.__setitem__("k", relayout(1, st["k"], k4_ref, k16_ref)),
        lambda: st.__setitem__("v", proj(c0 + 2 * ATTN_W)),
        lambda: st.__setitem__("v", relayout(2, st["v"], v4_ref, v16_ref)),
    ]
    keys = ("q", "q", "q", "k", "k", "k", "v", "v")
    n_conv = ts // CONV_ROWS
    every = n_conv // len(tasks)
    assert n_conv % len(tasks) == 0

    anchor = None
    for rc in range(n_conv):
        accs = []
        for lc in range(CONV_CH // CONV_LANES):
            ls = slice(lc * CONV_LANES, (lc + 1) * CONV_LANES)
            acc = jnp.broadcast_to(cb_ref[:, ls], (CONV_ROWS, CONV_LANES))
            if anchor is not None:
                acc = acc + jnp.minimum(jnp.abs(anchor[0:1, 0:1]), 0.0)
                anchor = None
            for k in range(CONV_K):
                blk, p = divmod(CONV_K - 1 - k, SUBLANES)
                g0 = rc * CONV_ROWS + CONV_HALO - blk * SUBLANES
                src = (gbuf[g0:g0 + CONV_ROWS, ls] if p == 0
                       else sbuf[p - 1, g0:g0 + CONV_ROWS, ls])
                acc = acc + cw_ref[k:k + 1, ls] * src
            accs.append(acc)
        acc = jnp.concatenate(accs, axis=1)
        mu = jnp.mean(acc, axis=-1, keepdims=True)
        d = acc - mu
        var = jnp.mean(d * d, axis=-1, keepdims=True)
        y = d * lax.rsqrt(var + EPS) * cng_ref[...] + cnb_ref[...]
        u_ref[0, rc * CONV_ROWS:(rc + 1) * CONV_ROWS, :] = (y * jax.nn.sigmoid(y)).astype(BF16)
        if rc % every == 0:
            t = rc // every
            tasks[t]()
            anchor = st[keys[t]]

    gbuf[0:CONV_HALO, :] = gbuf[ts:ts + CONV_HALO, :]


def _mix_in(x, g1, win, cw, cb, cng, cnb, qg, kg, hsum):
    B, S, _ = x.shape
    ts = ROW_TILE
    n_in = win.shape[1]
    tok = lambda b, i: (b, i, 0)
    strm = lambda b, i: (b, 0, i, 0)
    m4 = jax.ShapeDtypeStruct((B, 4, S // 4, ATTN_W), BF16)
    m16 = jax.ShapeDtypeStruct((B, 16, S // 16, ATTN_W), BF16)
    m4_spec = pl.BlockSpec((1, 4, ts // 4, ATTN_W), strm)
    m16_spec = pl.BlockSpec((1, 16, ts // 16, ATTN_W), strm)
    return pl.pallas_call(
        _mix_in_kernel,
        grid=(B, S // ts),
        in_specs=[
            pl.BlockSpec((1, ts, D_MODEL), tok),
            _resident((1, D_MODEL)),
            _resident((D_MODEL, n_in)),
            _resident((CONV_K, CONV_CH)),
            _resident((1, CONV_CH)),
            _resident((1, CONV_CH)),
            _resident((1, CONV_CH)),
            _resident((1, ATTN_W)),
            _resident((1, ATTN_W)),
            _resident((ATTN_W, ATTN_W)),
        ],
        out_specs=[pl.BlockSpec((1, ts, CONV_CH), tok),
                   m4_spec, m4_spec, m4_spec, m16_spec, m16_spec, m16_spec],
        out_shape=[jax.ShapeDtypeStruct((B, S, CONV_CH), BF16), m4, m4, m4, m16, m16, m16],
        scratch_shapes=[pltpu.VMEM((ts + CONV_HALO, CONV_CH), F32),
                        pltpu.VMEM((SUBLANES - 1, ts + CONV_HALO, CONV_CH), F32),
                        pltpu.VMEM((3, ATTN_W // LANES, ts, LANES), F32),
                        pltpu.VMEM((3, ATTN_W // LANES, ts, LANES), F32)],
        compiler_params=pltpu.CompilerParams(
            dimension_semantics=("arbitrary", "arbitrary"), vmem_limit_bytes=VMEM_LIMIT),
        name="mix_in",
    )(x, g1, win, cw, cb, cng, cnb, qg, kg, hsum)


def _attn_kernel(slopes_ref, q4_ref, k4_ref, v4_ref, q16_ref, k16_ref, v16_ref, o_ref,
                 bias_ref, acc_s, m_s, l_s, *, seq):
    pair = pl.program_id(1)
    sb = pl.program_id(2)
    qb = Q_BLOCK
    sub = qb // 4
    l4 = seq // 4
    l16 = seq // 16
    s4 = SUPER // 4
    lane = lax.broadcasted_iota(jnp.int32, (qb, LANES), 1)
    is_a = lane < HEAD_DIM
    mask_a = is_a.astype(BF16)
    mask_b = (~is_a).astype(BF16)
    ones_cols = jnp.ones((2 * qb, LANES), BF16)

    @pl.when(sb == 0)
    def _():
        row = lax.broadcasted_iota(jnp.int32, (2 * qb, 2 * qb), 0)
        col = lax.broadcasted_iota(jnp.int32, (2 * qb, 2 * qb), 1)
        qi = row & (qb - 1)
        ci = col & (qb - 1)
        slope = jnp.where(row < qb, slopes_ref[2 * pair], slopes_ref[2 * pair + 1])
        cur = col >= qb
        d_stream = qi + qb - col
        sub_log2 = sub.bit_length() - 1
        tq = 4 * (qi & (sub - 1)) + (qi >> sub_log2)
        tk = 4 * (ci & (sub - 1)) + (ci >> sub_log2) + jnp.where(cur, 0, -qb)
        d_tok = tq - tk
        for pat, (delta, dil) in enumerate(((d_tok, 1), (d_stream, 4), (d_stream, 16))):
            ok = (delta >= 0) & (delta <= N_BACK)
            b = jnp.where(ok, -slope * (delta * dil).astype(F32) * LOG2E, NEG)
            bias_ref[pat, 0] = b
            bias_ref[pat, 1] = jnp.where(cur, b, NEG)

    def unit(q, kc, vc, pat, fi):
        q2 = jnp.concatenate([q * mask_a, q * mask_b], axis=0)
        s = lax.dot_general(q2, kc, (((1,), (1,)), ((), ())), preferred_element_type=F32)
        s = s + bias_ref[pat, fi]
        m = jnp.max(jnp.maximum(s[:, :qb], s[:, qb:]), axis=1, keepdims=True)
        p = jnp.exp2(s - m).astype(BF16)
        pv = jnp.dot(p, jnp.concatenate([vc, ones_cols], axis=1), preferred_element_type=F32)
        acc = jnp.where(is_a, pv[:qb, :LANES], pv[qb:, :LANES])
        l = jnp.where(is_a, pv[:qb, LANES:], pv[qb:, LANES:])
        mm = jnp.where(is_a, m[:qb], m[qb:])
        return acc, mm, l

    def merge(acc0, m0, l0, acc1, m1, l1):
        mn = jnp.maximum(m0, m1)
        a = jnp.exp2(m0 - mn)
        b = jnp.exp2(m1 - mn)
        return a * acc0 + b * acc1, mn, a * l0 + b * l1

    def rows(ref, start, n):
        return ref[0, pl.ds(pl.multiple_of(start, BF16_ROWS), n), :]

    first_sb = (sb == 0).astype(jnp.int32)
    n_units = SUPER // qb

    def block(ref, q0):
        return jnp.concatenate([rows(ref, jnp.maximum(q0 - qb, 0), qb), rows(ref, q0, qb)], axis=0)

    for j in range(n_units):
        r4, mb = divmod(j, 4)
        q0 = r4 * l4 + sb * s4 + mb * qb
        acc, m, l = unit(rows(q4_ref, q0, qb), block(k4_ref, q0), block(v4_ref, q0), 1,
                         first_sb if mb == 0 else 0)
        dst = pl.ds(r4 * s4 + mb * qb, qb)
        acc_s[dst, :] = acc
        m_s[dst, :] = m
        l_s[dst, :] = l

    for r16 in range(n_units):
        q0 = r16 * l16 + sb * qb
        acc, m, l = unit(rows(q16_ref, q0, qb), block(k16_ref, q0), block(v16_ref, q0), 2,
                         first_sb)
        dst = pl.ds((r16 % 4) * s4 + r16 // 4, qb, stride=4)
        acc, m, l = merge(acc_s[dst, :], m_s[dst, :], l_s[dst, :], acc, m, l)
        acc_s[dst, :] = acc
        m_s[dst, :] = m
        l_s[dst, :] = l

    for nb in range(n_units):
        piece = lambda ref, off: jnp.concatenate(
            [rows(ref, jnp.maximum(a * l4 + sb * s4 + nb * sub + off, 0), sub) for a in range(4)],
            axis=0)
        kc = jnp.concatenate([piece(k4_ref, -sub), piece(k4_ref, 0)], axis=0)
        vc = jnp.concatenate([piece(v4_ref, -sub), piece(v4_ref, 0)], axis=0)
        acc, m, l = unit(piece(q4_ref, 0), kc, vc, 0, first_sb if nb == 0 else 0)
        gather = lambda ref: jnp.concatenate(
            [ref[a * s4 + nb * sub:a * s4 + (nb + 1) * sub, :] for a in range(4)], axis=0)
        acc, m, l = merge(gather(acc_s), gather(m_s), gather(l_s), acc, m, l)
        o = (acc / l).astype(BF16)
        for a in range(4):
            o_ref[0, a, nb * sub:(nb + 1) * sub, :] = o[a * sub:(a + 1) * sub]


def _attn(slopes, q4, k4, v4, q16, k16, v16):
    B, S, _ = q4.shape
    seq_spec = pl.BlockSpec((1, S, LANES), lambda b, p, sb, *_: (b, 0, p))
    return pl.pallas_call(
        functools.partial(_attn_kernel, seq=S),
        grid_spec=pltpu.PrefetchScalarGridSpec(
            num_scalar_prefetch=1,
            grid=(B, ATTN_W // LANES, S // SUPER),
            in_specs=[seq_spec] * 6,
            out_specs=pl.BlockSpec((1, 4, SUPER // 4, LANES), lambda b, p, sb, *_: (b, 0, sb, p)),
            scratch_shapes=[pltpu.VMEM((len(PATTERNS), 2, 2 * Q_BLOCK, 2 * Q_BLOCK), F32),
                            pltpu.VMEM((SUPER, LANES), F32),
                            pltpu.VMEM((SUPER, LANES), F32),
                            pltpu.VMEM((SUPER, LANES), F32)],
        ),
        out_shape=jax.ShapeDtypeStruct((B, 4, S // 4, ATTN_W), BF16),
        compiler_params=pltpu.CompilerParams(
            dimension_semantics=("arbitrary", "arbitrary", "arbitrary"),
            vmem_limit_bytes=VMEM_LIMIT),
        name="attn",
    )(slopes, q4, k4, v4, q16, k16, v16)


def _ffn_kernel(x_ref, u_ref, o4_ref, wout_ref, g2_ref, wup_ref, fcw_ref, fcb_ref, wdn_ref,
                out_ref, hbuf, upbuf, obuf, actbuf):
    ts = ROW_TILE
    i = pl.program_id(1)
    for c in range(ATTN_W // LANES):
        for r in range(4):
            obuf[c, pl.ds(r, ts // 4, stride=4), :] = (
                o4_ref[0, r, :, c * LANES:(c + 1) * LANES].astype(F32))
    @pl.when(i == 0)
    def _():
        hbuf[0:FFN_HALO, :] = jnp.zeros((FFN_HALO, D_MODEL), BF16)

    blk = ts // FRONT_BLOCKS
    for r in range(FRONT_BLOCKS):
        rs = slice(r * blk, (r + 1) * blk)
        o = jnp.concatenate([obuf[c, rs, :] for c in range(ATTN_W // LANES)], axis=1).astype(BF16)
        x1 = (x_ref[0, rs, :]
              + jnp.dot(u_ref[0, rs, :], wout_ref[0:CONV_CH, :], preferred_element_type=F32)
              + jnp.dot(o, wout_ref[CONV_CH:CONV_CH + ATTN_W, :], preferred_element_type=F32))
        out_ref[0, rs, :] = x1
        h2 = x1 * lax.rsqrt(jnp.mean(x1 * x1, axis=-1, keepdims=True) + EPS) * g2_ref[...]
        hbuf[FFN_HALO + r * blk:FFN_HALO + (r + 1) * blk, :] = h2.astype(BF16)

    he = hbuf[...]

    def up_proj(c):
        for j, col0 in enumerate((c * FFN_COLS, D_FF + c * FFN_COLS)):
            upbuf[2 * (c % 2) + j] = jnp.dot(he, wup_ref[:, col0:col0 + FFN_COLS],
                                             preferred_element_type=F32)

    def conv(slot, col0):
        out = jnp.broadcast_to(fcb_ref[:, col0:col0 + FFN_COLS], (ts, FFN_COLS))
        for k in range(FFN_K):
            r0 = FFN_HALO - (FFN_K - 1) + k
            out = out + fcw_ref[k:k + 1, col0:col0 + FFN_COLS] * upbuf[slot, r0:r0 + ts, :]
        return out

    def gate_mul(c):
        gate = conv(2 * (c % 2), c * FFN_COLS)
        val = conv(2 * (c % 2) + 1, D_FF + c * FFN_COLS)
        actbuf[:, c * FFN_COLS:(c + 1) * FFN_COLS] = (gate * jax.nn.sigmoid(gate) * val).astype(BF16)

    def down_proj(c_lo, c_hi):
        out_ref[0] += jnp.dot(actbuf[:, c_lo * FFN_COLS:c_hi * FFN_COLS],
                              wdn_ref[c_lo * FFN_COLS:c_hi * FFN_COLS, :],
                              preferred_element_type=F32)

    n_chunks = D_FF // FFN_COLS
    up_proj(0)
    pending = 0
    for c in range(n_chunks):
        if c + 1 < n_chunks:
            up_proj(c + 1)
        gate_mul(c)
        if c - pending >= DOWN_CHUNKS:
            down_proj(pending, pending + DOWN_CHUNKS)
            pending += DOWN_CHUNKS
    down_proj(pending, n_chunks)
    hbuf[0:FFN_HALO, :] = hbuf[ts:ts + FFN_HALO, :]


def _ffn(x, u, o4, wout, g2, wup, fcw, fcb, wdn):
    B, S, _ = x.shape
    ts = ROW_TILE
    tok = lambda b, i: (b, i, 0)
    return pl.pallas_call(
        _ffn_kernel,
        grid=(B, S // ts),
        in_specs=[
            pl.BlockSpec((1, ts, D_MODEL), tok),
            pl.BlockSpec((1, ts, CONV_CH), tok),
            pl.BlockSpec((1, 4, ts // 4, ATTN_W), lambda b, i: (b, 0, i, 0)),
            _resident((CONV_CH + ATTN_W, D_MODEL)),
            _resident((1, D_MODEL)),
            _resident((D_MODEL, 2 * D_FF)),
            _resident((FFN_K, 2 * D_FF)),
            _resident((1, 2 * D_FF)),
            _resident((D_FF, D_MODEL)),
        ],
        out_specs=pl.BlockSpec((1, ts, D_MODEL), tok),
        out_shape=jax.ShapeDtypeStruct((B, S, D_MODEL), F32),
        scratch_shapes=[pltpu.VMEM((ts + FFN_HALO, D_MODEL), BF16),
                        pltpu.VMEM((4, ts + FFN_HALO, FFN_COLS), F32),
                        pltpu.VMEM((ATTN_W // LANES, ts, LANES), F32),
                        pltpu.VMEM((ts, D_FF), BF16)],
        compiler_params=pltpu.CompilerParams(
            dimension_semantics=("arbitrary", "arbitrary"), vmem_limit_bytes=VMEM_LIMIT),
        name="ffn",
    )(x, u, o4, wout, g2, wup, fcw, fcb, wdn)


def kernel(x, norm1_g, w_in, conv_w, conv_b, cn_g, cn_b, q_norm_g, k_norm_g, w_out, norm2_g,
           w_up, ffconv_w, ffconv_b, w_down):
    B, S, D = x.shape
    assert D == D_MODEL and S % SUPER == 0 and S % ROW_TILE == 0
    assert all(w // d == N_BACK for w, d in PATTERNS) and N_BACK == Q_BLOCK
    assert D_FF % FFN_COLS == 0
    row = lambda v: v.reshape(1, -1).astype(F32)
    head = jnp.arange(ATTN_W) // HEAD_DIM
    hsum = (head[:, None] == head[None, :]).astype(BF16)
    slopes = 2.0 ** (-8.0 * jnp.arange(1, N_HEADS + 1, dtype=F32) / N_HEADS)

    u, q4, k4, v4, q16, k16, v16 = _mix_in(
        x, row(norm1_g), w_in.astype(BF16), conv_w.astype(F32), row(conv_b), row(cn_g),
        row(cn_b), row(jnp.tile(q_norm_g, N_HEADS)), row(jnp.tile(k_norm_g, N_HEADS)), hsum)
    flat = lambda a: a.reshape(B, S, ATTN_W)
    o4 = _attn(slopes, flat(q4), flat(k4), flat(v4), flat(q16), flat(k16), flat(v16))
    return _ffn(x, u, o4, w_out.astype(BF16), row(norm2_g), w_up.astype(BF16),
                ffconv_w.astype(F32), row(ffconv_b), w_down.astype(BF16))
```

```python
import functools

import jax
import jax.numpy as jnp
from jax import lax
from jax.experimental import pallas as pl
from jax.experimental.pallas import tpu as pltpu

F32 = jnp.float32
BF16 = jnp.bfloat16

D_MODEL = 1024
HEAD_DIM = 64
N_HEADS = 8
CONV_CH = 512
ATTN_W = N_HEADS * HEAD_DIM
CONV_K = 31
D_FF = 2816
FFN_K = 3
EPS = 1e-6
PATTERNS = ((128, 1), (512, 4), (2048, 16))
Q_BLOCK = 128
N_BACK = 128
NEG = -1e30
LOG2E = 1.4426950408889634

LANES = 128
SUBLANES = 8
BF16_ROWS = 16
ROW_TILE = 512
CONV_HALO = 32
CONV_ROWS = 64
CONV_LANES = 128
FFN_HALO = 16
FFN_COLS = 256
DOWN_CHUNKS = 2
FRONT_BLOCKS = 4
FFN_FRONT_BLOCKS = 2
SUPER = 16 * Q_BLOCK
VMEM_LIMIT = 52 * 1024 * 1024


def _resident(shape):
    return pl.BlockSpec(shape, lambda *_: (0,) * len(shape), pipeline_mode=pl.Buffered(1))


def _mix_in_kernel(x_ref, g1_ref, win_ref, cw_ref, cb_ref, cng_ref, cnb_ref, qg_ref, kg_ref,
                   hsum_ref,
                   u_ref, q4_ref, k4_ref, v4_ref, q16_ref, k16_ref, v16_ref,
                   gbuf, sbuf, zbuf, pbuf4):
    ts = ROW_TILE
    i = pl.program_id(1)

    @pl.when(i == 0)
    def _():
        gbuf[0:CONV_HALO, :] = jnp.zeros((CONV_HALO, CONV_CH), F32)

    blk = ts // FRONT_BLOCKS
    hs = []
    for r in range(FRONT_BLOCKS):
        x = x_ref[0, r * blk:(r + 1) * blk, :]
        hr = (x * lax.rsqrt(jnp.mean(x * x, axis=-1, keepdims=True) + EPS) * g1_ref[...]).astype(BF16)
        a = jnp.dot(hr, win_ref[:, 0:2 * CONV_CH], preferred_element_type=F32)
        gbuf[CONV_HALO + r * blk:CONV_HALO + (r + 1) * blk, :] = (
            a[:, :CONV_CH] * jax.nn.sigmoid(a[:, CONV_CH:]))
        hs.append(hr)
    h = jnp.concatenate(hs, axis=0)

    lo = SUBLANES
    for p in range(1, SUBLANES):
        sbuf[p - 1, lo:ts + CONV_HALO, :] = gbuf[lo - p:ts + CONV_HALO - p, :]

    def proj(col0):
        return jnp.dot(h, win_ref[:, col0:col0 + ATTN_W], preferred_element_type=F32)

    def head_norm(z, g_ref):
        ss = jnp.dot((z * z).astype(BF16), hsum_ref[...], preferred_element_type=F32)
        return z * lax.rsqrt(ss * (1.0 / HEAD_DIM) + EPS) * g_ref[...]

    def relayout(slot, val, m4_ref, m16_ref):
        n4 = ts // 4
        n16 = ts // 16
        for c in range(ATTN_W // LANES):
            cols = slice(c * LANES, (c + 1) * LANES)
            zbuf[slot, c] = val[:, cols]
            for r in range(4):
                piece = zbuf[slot, c, pl.ds(r, n4, stride=4), :]
                m4_ref[0, r, :, cols] = piece.astype(BF16)
                pbuf4[slot, c, r * n4:(r + 1) * n4, :] = piece
            for r16 in range(16):
                piece = pbuf4[slot, c, pl.ds((r16 % 4) * n4 + r16 // 4, n16, stride=4), :]
                m16_ref[0, r16, :, cols] = piece.astype(BF16)
        return piece

    c0 = 2 * CONV_CH
    st = {}
    tasks = [
        lambda: st.__setitem__("q", proj(c0)),
        lambda: st.__setitem__("q", head_norm(st["q"], qg_ref) * (HEAD_DIM ** -0.5 * LOG2E)),
        lambda: st.__setitem__("q", relayout(0, st["q"], q4_ref, q16_ref)),
        lambda: st.__setitem__("k", proj(c0 + ATTN_W)),
        lambda: st.__setitem__("k", head_norm(st["k"], kg_ref)),
        lambda: st.__setitem__("k", relayout(1, st["k"], k4_ref, k16_ref)),
        lambda: st.__setitem__("v", proj(c0 + 2 * ATTN_W)),
        lambda: st.__setitem__("v", relayout(2, st["v"], v4_ref, v16_ref)),
    ]
    keys = ("q", "q", "q", "k", "k", "k", "v", "v")
    n_conv = ts // CONV_ROWS
    every = n_conv // len(tasks)
    assert n_conv % len(tasks) == 0

    anchor = None
    for rc in range(n_conv):
        accs = []
        for lc in range(CONV_CH // CONV_LANES):
            ls = slice(lc * CONV_LANES, (lc + 1) * CONV_LANES)
            acc = jnp.broadcast_to(cb_ref[:, ls], (CONV_ROWS, CONV_LANES))
            if anchor is not None:
                acc = acc + jnp.minimum(jnp.abs(anchor[0:1, 0:1]), 0.0)
                anchor = None
            for k in range(CONV_K):
                blk, p = divmod(CONV_K - 1 - k, SUBLANES)
                g0 = rc * CONV_ROWS + CONV_HALO - blk * SUBLANES
                src = (gbuf[g0:g0 + CONV_ROWS, ls] if p == 0
                       else sbuf[p - 1, g0:g0 + CONV_ROWS, ls])
                acc = acc + cw_ref[k:k + 1, ls] * src
            accs.append(acc)
        acc = jnp.concatenate(accs, axis=1)
        mu = jnp.mean(acc, axis=-1, keepdims=True)
        d = acc - mu
        var = jnp.mean(d * d, axis=-1, keepdims=True)
        y = d * lax.rsqrt(var + EPS) * cng_ref[...] + cnb_ref[...]
        u_ref[0, rc * CONV_ROWS:(rc + 1) * CONV_ROWS, :] = (y * jax.nn.sigmoid(y)).astype(BF16)
        if rc % every == 0:
            t = rc // every
            tasks[t]()
            anchor = st[keys[t]]

    gbuf[0:CONV_HALO, :] = gbuf[ts:ts + CONV_HALO, :]


def _mix_in(x, g1, win, cw, cb, cng, cnb, qg, kg, hsum):
    B, S, _ = x.shape
    ts = ROW_TILE
    n_in = win.shape[1]
    tok = lambda b, i: (b, i, 0)
    strm = lambda b, i: (b, 0, i, 0)
    m4 = jax.ShapeDtypeStruct((B, 4, S // 4, ATTN_W), BF16)
    m16 = jax.ShapeDtypeStruct((B, 16, S // 16, ATTN_W), BF16)
    m4_spec = pl.BlockSpec((1, 4, ts // 4, ATTN_W), strm)
    m16_spec = pl.BlockSpec((1, 16, ts // 16, ATTN_W), strm)
    return pl.pallas_call(
        _mix_in_kernel,
        grid=(B, S // ts),
        in_specs=[
            pl.BlockSpec((1, ts, D_MODEL), tok),
            _resident((1, D_MODEL)),
            _resident((D_MODEL, n_in)),
            _resident((CONV_K, CONV_CH)),
            _resident((1, CONV_CH)),
            _resident((1, CONV_CH)),
            _resident((1, CONV_CH)),
            _resident((1, ATTN_W)),
            _resident((1, ATTN_W)),
            _resident((ATTN_W, ATTN_W)),
        ],
        out_specs=[pl.BlockSpec((1, ts, CONV_CH), tok),
                   m4_spec, m4_spec, m4_spec, m16_spec, m16_spec, m16_spec],
        out_shape=[jax.ShapeDtypeStruct((B, S, CONV_CH), BF16), m4, m4, m4, m16, m16, m16],
        scratch_shapes=[pltpu.VMEM((ts + CONV_HALO, CONV_CH), F32),
                        pltpu.VMEM((SUBLANES - 1, ts + CONV_HALO, CONV_CH), F32),
                        pltpu.VMEM((3, ATTN_W // LANES, ts, LANES), F32),
                        pltpu.VMEM((3, ATTN_W // LANES, ts, LANES), F32)],
        compiler_params=pltpu.CompilerParams(
            dimension_semantics=("arbitrary", "arbitrary"), vmem_limit_bytes=VMEM_LIMIT),
        name="mix_in",
    )(x, g1, win, cw, cb, cng, cnb, qg, kg, hsum)


def _attn_kernel(slopes_ref, q4_ref, k4_ref, v4_ref, q16_ref, k16_ref, v16_ref, o_ref,
                 bias_ref, acc_s, m_s, l_s, *, seq):
    pair = pl.program_id(1)
    sb = pl.program_id(2)
    qb = Q_BLOCK
    sub = qb // 4
    l4 = seq // 4
    l16 = seq // 16
    s4 = SUPER // 4
    lane = lax.broadcasted_iota(jnp.int32, (qb, LANES), 1)
    is_a = lane < HEAD_DIM
    mask_a = is_a.astype(BF16)
    mask_b = (~is_a).astype(BF16)
    ones_cols = jnp.ones((2 * qb, LANES), BF16)

    @pl.when(sb == 0)
    def _():
        row = lax.broadcasted_iota(jnp.int32, (2 * qb, 2 * qb), 0)
        col = lax.broadcasted_iota(jnp.int32, (2 * qb, 2 * qb), 1)
        qi = row & (qb - 1)
        ci = col & (qb - 1)
        slope = jnp.where(row < qb, slopes_ref[2 * pair], slopes_ref[2 * pair + 1])
        cur = col >= qb
        d_stream = qi + qb - col
        sub_log2 = sub.bit_length() - 1
        tq = 4 * (qi & (sub - 1)) + (qi >> sub_log2)
        tk = 4 * (ci & (sub - 1)) + (ci >> sub_log2) + jnp.where(cur, 0, -qb)
        d_tok = tq - tk
        for pat, (delta, dil) in enumerate(((d_tok, 1), (d_stream, 4), (d_stream, 16))):
            ok = (delta >= 0) & (delta <= N_BACK)
            b = jnp.where(ok, -slope * (delta * dil).astype(F32) * LOG2E, NEG)
            bias_ref[pat, 0] = b
            bias_ref[pat, 1] = jnp.where(cur, b, NEG)

    def unit(q, kc, vc, pat, fi):
        q2 = jnp.concatenate([q * mask_a, q * mask_b], axis=0)
        s = lax.dot_general(q2, kc, (((1,), (1,)), ((), ())), preferred_element_type=F32)
        s = s + bias_ref[pat, fi]
        m = jnp.max(jnp.maximum(s[:, :qb], s[:, qb:]), axis=1, keepdims=True)
        p = jnp.exp2(s - m).astype(BF16)
        pv = jnp.dot(p, jnp.concatenate([vc, ones_cols], axis=1), preferred_element_type=F32)
        acc = jnp.where(is_a, pv[:qb, :LANES], pv[qb:, :LANES])
        l = jnp.where(is_a, pv[:qb, LANES:], pv[qb:, LANES:])
        mm = jnp.where(is_a, m[:qb], m[qb:])
        return acc, mm, l

    def merge(acc0, m0, l0, acc1, m1, l1):
        mn = jnp.maximum(m0, m1)
        a = jnp.exp2(m0 - mn)
        b = jnp.exp2(m1 - mn)
        return a * acc0 + b * acc1, mn, a * l0 + b * l1

    def rows(ref, start, n):
        return ref[0, pl.ds(pl.multiple_of(start, BF16_ROWS), n), :]

    first_sb = (sb == 0).astype(jnp.int32)
    n_units = SUPER // qb

    def block(ref, q0):
        return jnp.concatenate([rows(ref, jnp.maximum(q0 - qb, 0), qb), rows(ref, q0, qb)], axis=0)

    for j in range(n_units):
        r4, mb = divmod(j, 4)
        q0 = r4 * l4 + sb * s4 + mb * qb
        acc, m, l = unit(rows(q4_ref, q0, qb), block(k4_ref, q0), block(v4_ref, q0), 1,
                         first_sb if mb == 0 else 0)
        dst = pl.ds(r4 * s4 + mb * qb, qb)
        acc_s[dst, :] = acc
        m_s[dst, :] = m
        l_s[dst, :] = l

    for r16 in range(n_units):
        q0 = r16 * l16 + sb * qb
        acc, m, l = unit(rows(q16_ref, q0, qb), block(k16_ref, q0), block(v16_ref, q0), 2,
                         first_sb)
        dst = pl.ds((r16 % 4) * s4 + r16 // 4, qb, stride=4)
        acc, m, l = merge(acc_s[dst, :], m_s[dst, :], l_s[dst, :], acc, m, l)
        acc_s[dst, :] = acc
        m_s[dst, :] = m
        l_s[dst, :] = l

    for nb in range(n_units):
        piece = lambda ref, off: jnp.concatenate(
            [rows(ref, jnp.maximum(a * l4 + sb * s4 + nb * sub + off, 0), sub) for a in range(4)],
            axis=0)
        kc = jnp.concatenate([piece(k4_ref, -sub), piece(k4_ref, 0)], axis=0)
        vc = jnp.concatenate([piece(v4_ref, -sub), piece(v4_ref, 0)], axis=0)
        acc, m, l = unit(piece(q4_ref, 0), kc, vc, 0, first_sb if nb == 0 else 0)
        gather = lambda ref: jnp.concatenate(
            [ref[a * s4 + nb * sub:a * s4 + (nb + 1) * sub, :] for a in range(4)], axis=0)
        acc, m, l = merge(gather(acc_s), gather(m_s), gather(l_s), acc, m, l)
        o = (acc / l).astype(BF16)
        for a in range(4):
            o_ref[0, a, nb * sub:(nb + 1) * sub, :] = o[a * sub:(a + 1) * sub]


def _attn(slopes, q4, k4, v4, q16, k16, v16):
    B, S, _ = q4.shape
    seq_spec = pl.BlockSpec((1, S, LANES), lambda b, p, sb, *_: (b, 0, p))
    return pl.pallas_call(
        functools.partial(_attn_kernel, seq=S),
        grid_spec=pltpu.PrefetchScalarGridSpec(
            num_scalar_prefetch=1,
            grid=(B, ATTN_W // LANES, S // SUPER),
            in_specs=[seq_spec] * 6,
            out_specs=pl.BlockSpec((1, 4, SUPER // 4, LANES), lambda b, p, sb, *_: (b, 0, sb, p)),
            scratch_shapes=[pltpu.VMEM((len(PATTERNS), 2, 2 * Q_BLOCK, 2 * Q_BLOCK), F32),
                            pltpu.VMEM((SUPER, LANES), F32),
                            pltpu.VMEM((SUPER, LANES), F32),
                            pltpu.VMEM((SUPER, LANES), F32)],
        ),
        out_shape=jax.ShapeDtypeStruct((B, 4, S // 4, ATTN_W), BF16),
        compiler_params=pltpu.CompilerParams(
            dimension_semantics=("arbitrary", "arbitrary", "arbitrary"),
            vmem_limit_bytes=VMEM_LIMIT),
        name="attn",
    )(slopes, q4, k4, v4, q16, k16, v16)


def _ffn_kernel(x_ref, u_ref, o4_ref, wout_ref, g2_ref, wup_ref, fcw_ref, fcb_ref, wdn_ref,
                out_ref, hbuf, halo, upbuf, obuf, actbuf):
    ts = ROW_TILE
    i = pl.program_id(1)
    for c in range(ATTN_W // LANES):
        for r in range(4):
            obuf[c, pl.ds(r, ts // 4, stride=4), :] = (
                o4_ref[0, r, :, c * LANES:(c + 1) * LANES].astype(F32))
    @pl.when(i == 0)
    def _():
        halo[...] = jnp.zeros(halo.shape, F32)

    blk = ts // FFN_FRONT_BLOCKS
    for r in range(FFN_FRONT_BLOCKS):
        rs = slice(r * blk, (r + 1) * blk)
        o = jnp.concatenate([obuf[c, rs, :] for c in range(ATTN_W // LANES)], axis=1).astype(BF16)
        x1 = (x_ref[0, rs, :]
              + jnp.dot(u_ref[0, rs, :], wout_ref[0:CONV_CH, :], preferred_element_type=F32)
              + jnp.dot(o, wout_ref[CONV_CH:CONV_CH + ATTN_W, :], preferred_element_type=F32))
        out_ref[0, rs, :] = x1
        h2 = x1 * lax.rsqrt(jnp.mean(x1 * x1, axis=-1, keepdims=True) + EPS) * g2_ref[...]
        hbuf[r * blk:(r + 1) * blk, :] = h2.astype(BF16)

    he = hbuf[...]

    def up_proj(c):
        for j, col0 in enumerate((c * FFN_COLS, D_FF + c * FFN_COLS)):
            slot = 2 * (c % 2) + j
            upbuf[slot, FFN_HALO - SUBLANES:FFN_HALO, :] = halo[2 * c + j]
            up = jnp.dot(he, wup_ref[:, col0:col0 + FFN_COLS], preferred_element_type=F32)
            upbuf[slot, FFN_HALO:FFN_HALO + ts, :] = up
            halo[2 * c + j] = up[ts - SUBLANES:ts, :]

    def conv(slot, col0):
        out = jnp.broadcast_to(fcb_ref[:, col0:col0 + FFN_COLS], (ts, FFN_COLS))
        for k in range(FFN_K):
            r0 = FFN_HALO - (FFN_K - 1) + k
            out = out + fcw_ref[k:k + 1, col0:col0 + FFN_COLS] * upbuf[slot, r0:r0 + ts, :]
        return out

    def gate_mul(c):
        gate = conv(2 * (c % 2), c * FFN_COLS)
        val = conv(2 * (c % 2) + 1, D_FF + c * FFN_COLS)
        actbuf[:, c * FFN_COLS:(c + 1) * FFN_COLS] = (gate * jax.nn.sigmoid(gate) * val).astype(BF16)

    def down_proj(c_lo, c_hi):
        out_ref[0] += jnp.dot(actbuf[:, c_lo * FFN_COLS:c_hi * FFN_COLS],
                              wdn_ref[c_lo * FFN_COLS:c_hi * FFN_COLS, :],
                              preferred_element_type=F32)

    n_chunks = D_FF // FFN_COLS
    up_proj(0)
    pending = 0
    for c in range(n_chunks):
        if c + 1 < n_chunks:
            up_proj(c + 1)
        gate_mul(c)
        if c - pending >= DOWN_CHUNKS:
            down_proj(pending, pending + DOWN_CHUNKS)
            pending += DOWN_CHUNKS
    down_proj(pending, n_chunks)


def _ffn(x, u, o4, wout, g2, wup, fcw, fcb, wdn):
    B, S, _ = x.shape
    ts = ROW_TILE
    tok = lambda b, i: (b, i, 0)
    return pl.pallas_call(
        _ffn_kernel,
        grid=(B, S // ts),
        in_specs=[
            pl.BlockSpec((1, ts, D_MODEL), tok),
            pl.BlockSpec((1, ts, CONV_CH), tok),
            pl.BlockSpec((1, 4, ts // 4, ATTN_W), lambda b, i: (b, 0, i, 0)),
            _resident((CONV_CH + ATTN_W, D_MODEL)),
            _resident((1, D_MODEL)),
            _resident((D_MODEL, 2 * D_FF)),
            _resident((FFN_K, 2 * D_FF)),
            _resident((1, 2 * D_FF)),
            _resident((D_FF, D_MODEL)),
        ],
        out_specs=pl.BlockSpec((1, ts, D_MODEL), tok),
        out_shape=jax.ShapeDtypeStruct((B, S, D_MODEL), F32),
        scratch_shapes=[pltpu.VMEM((ts, D_MODEL), BF16),
                        pltpu.VMEM((2 * D_FF // FFN_COLS, SUBLANES, FFN_COLS), F32),
                        pltpu.VMEM((4, ts + FFN_HALO, FFN_COLS), F32),
                        pltpu.VMEM((ATTN_W // LANES, ts, LANES), F32),
                        pltpu.VMEM((ts, D_FF), BF16)],
        compiler_params=pltpu.CompilerParams(
            dimension_semantics=("arbitrary", "arbitrary"), vmem_limit_bytes=VMEM_LIMIT),
        name="ffn",
    )(x, u, o4, wout, g2, wup, fcw, fcb, wdn)


def kernel(x, norm1_g, w_in, conv_w, conv_b, cn_g, cn_b, q_norm_g, k_norm_g, w_out, norm2_g,
           w_up, ffconv_w, ffconv_b, w_down):
    B, S, D = x.shape
    assert D == D_MODEL and S % SUPER == 0 and S % ROW_TILE == 0
    assert all(w // d == N_BACK for w, d in PATTERNS) and N_BACK == Q_BLOCK
    assert D_FF % FFN_COLS == 0
    row = lambda v: v.reshape(1, -1).astype(F32)
    head = jnp.arange(ATTN_W) // HEAD_DIM
    hsum = (head[:, None] == head[None, :]).astype(BF16)
    slopes = 2.0 ** (-8.0 * jnp.arange(1, N_HEADS + 1, dtype=F32) / N_HEADS)

    u, q4, k4, v4, q16, k16, v16 = _mix_in(
        x, row(norm1_g), w_in.astype(BF16), conv_w.astype(F32), row(conv_b), row(cn_g),
        row(cn_b), row(jnp.tile(q_norm_g, N_HEADS)), row(jnp.tile(k_norm_g, N_HEADS)), hsum)
    flat = lambda a: a.reshape(B, S, ATTN_W)
    o4 = _attn(slopes, flat(q4), flat(k4), flat(v4), flat(q16), flat(k16), flat(v16))
    return _ffn(x, u, o4, w_out.astype(BF16), row(norm2_g), w_up.astype(BF16),
                ffconv_w.astype(F32), row(ffconv_b), w_down.astype(BF16))
```
